```python
import jax, jax.numpy as jnp
from jax import lax
import numpy as np

D_MODEL = 1024
BATCH = 4
SEQ = 8192
DEPTH = 4
DEC_BATCH = 8
DEC_SEQ = 2048
PAST_LEN = 128

HEAD_DIM = 64
GRID_W = 64
A_HEADS = 8
A_KV_HEADS = 2
B_HEADS = 4
B_KV_HEADS = 2
C_HEADS = 4
Q_BLOCK = 128
WINDOW = 128
CHUNK = 128
ROPE_THETA = 10000.0
MEM_LEN = 256
X_HEADS = 4
X_HEAD_DIM = D_MODEL // X_HEADS
D_FF = ((8 * D_MODEL // 3 + 255) // 256) * 256
EPS = 1e-6
A_Q = A_HEADS * HEAD_DIM
A_KV = A_KV_HEADS * HEAD_DIM
B_Q = B_HEADS * HEAD_DIM
B_KV = B_KV_HEADS * HEAD_DIM
C_W = C_HEADS * HEAD_DIM
MIX_W = A_Q + B_Q + C_W
IN_SPLITS = (A_Q, A_KV, A_KV, B_Q, B_KV, B_KV, C_W, C_W, C_W, C_W)
IN_W = A_Q + 2 * A_KV + B_Q + 2 * B_KV + 4 * C_W
NEG_INF = -1e30

kernel_name = 'hybrid_bidir_parallel_heads_encoder'


def rmsnorm(x, g):
    xf = x.astype(jnp.float32)
    y = xf * lax.rsqrt(jnp.mean(xf * xf, axis=-1, keepdims=True) + EPS)
    return (y * g.astype(jnp.float32)).astype(x.dtype)


def rope_freqs(pos, dim):
    inv = ROPE_THETA ** (-jnp.arange(0, dim, 2, dtype=jnp.float32) / dim)
    return pos.astype(jnp.float32)[:, None] * inv[None, :]


def apply_rope(x, ang):
    half = x.shape[-1] // 2
    cos = jnp.cos(ang)[None, :, None, :]
    sin = jnp.sin(ang)[None, :, None, :]
    xf = x.astype(jnp.float32)
    x1, x2 = xf[..., :half], xf[..., half:]
    return jnp.concatenate([x1 * cos - x2 * sin, x1 * sin + x2 * cos], axis=-1).astype(x.dtype)


def axial_attention(q, k, v, g_q, g_k, ang):
    bsz, T, H, d = q.shape
    G = H // A_KV_HEADS
    q = apply_rope(rmsnorm(q, g_q), ang)
    k = apply_rope(rmsnorm(k, g_k), ang)
    scale = d ** -0.5
    qb = q.reshape(bsz, T // Q_BLOCK, Q_BLOCK, A_KV_HEADS, G, d).transpose(1, 0, 2, 3, 4, 5)

    def one_block(qi):
        s = jnp.einsum('bqkgd,bskd->bkgqs', qi, k, preferred_element_type=jnp.float32) * scale
        p = jax.nn.softmax(s, axis=-1).astype(v.dtype)
        return jnp.einsum('bkgqs,bskd->bqkgd', p, v)

    o = lax.map(one_block, qb)
    return o.transpose(1, 0, 2, 3, 4, 5).reshape(bsz, T, H * d)


def window_attention(q, k, v, sink, ang):
    bsz, T, H, d = q.shape
    K = B_KV_HEADS
    G = H // K
    W = WINDOW
    nb = T // W
    q = apply_rope(q, ang)
    k = apply_rope(k, ang)
    pad = ((0, 0), (W, W), (0, 0), (0, 0))
    kp = jnp.pad(k, pad).reshape(bsz, nb + 2, W, K, d)
    vp = jnp.pad(v, pad).reshape(bsz, nb + 2, W, K, d)
    kw = jnp.concatenate([kp[:, :-2], kp[:, 1:-1], kp[:, 2:]], axis=2)
    vw = jnp.concatenate([vp[:, :-2], vp[:, 1:-1], vp[:, 2:]], axis=2)
    qb = q.reshape(bsz, nb, W, K, G, d)
    s = jnp.einsum('bnqkgd,bnskd->bnkgqs', qb, kw, preferred_element_type=jnp.float32) * d ** -0.5
    blk = jnp.arange(nb)[:, None]
    qpos = blk * W + jnp.arange(W)[None, :]
    kpos = (blk - 1) * W + jnp.arange(3 * W)[None, :]
    rel = kpos[:, None, :] - qpos[:, :, None]
    valid = (jnp.abs(rel) <= W) & (kpos[:, None, :] >= 0) & (kpos[:, None, :] < T)
    s = jnp.where(valid[None, :, None, None], s, NEG_INF)
    sink_col = jnp.broadcast_to(sink.astype(jnp.float32).reshape(1, 1, K, G, 1, 1), s.shape[:-1] + (1,))
    p = jax.nn.softmax(jnp.concatenate([s, sink_col], axis=-1), axis=-1)[..., :-1].astype(v.dtype)
    o = jnp.einsum('bnkgqs,bnskd->bnqkgd', p, vw)
    return o.reshape(bsz, T, H * d)


def bidir_retention(q, k, v, gate, p_dec_f, p_dec_b, g_gn, ang):
    bsz, T, H, d = q.shape
    C = CHUNK
    nc = T // C
    f32 = jnp.float32
    qc = apply_rope(q, ang).astype(f32).reshape(bsz, nc, C, H, d)
    kc = (apply_rope(k, ang).astype(f32) * d ** -0.5).reshape(bsz, nc, C, H, d)
    vc = v.astype(f32).reshape(bsz, nc, C, H, d)
    lgf = -jnp.exp(p_dec_f.astype(f32))
    lgb = -jnp.exp(p_dec_b.astype(f32))
    idx = jnp.arange(C, dtype=f32)
    diff = idx[:, None] - idx[None, :]
    mask = jnp.where(diff >= 0,
                     jnp.exp(lgf[:, None, None] * jnp.maximum(diff, 0.0)),
                     jnp.exp(lgb[:, None, None] * jnp.maximum(-diff, 0.0)))
    s = jnp.einsum('bnahd,bnchd->bnhac', qc, kc) * mask
    o = jnp.einsum('bnhac,bnchd->bnahd', s, vc)
    zero = jnp.zeros((bsz, H, d, d), f32)
    kv_f = jnp.einsum('bnchd,hc,bnche->nbhde', kc, jnp.exp(lgf[:, None] * (C - 1 - idx)[None, :]), vc)
    dec_f = jnp.exp(lgf * C)[None, :, None, None]
    _, s_prev = lax.scan(lambda S, kv: (dec_f * S + kv, S), zero, kv_f)
    o = o + jnp.einsum('bnahd,ha,nbhde->bnahe', qc, jnp.exp(lgf[:, None] * (idx + 1.0)[None, :]), s_prev)
    kv_b = jnp.einsum('bnchd,hc,bnche->nbhde', kc, jnp.exp(lgb[:, None] * idx[None, :]), vc)
    dec_b = jnp.exp(lgb * C)[None, :, None, None]
    _, r_next = lax.scan(lambda R, kv: (dec_b * R + kv, R), zero, kv_b, reverse=True)
    o = o + jnp.einsum('bnahd,ha,nbhde->bnahe', qc, jnp.exp(lgb[:, None] * (C - idx)[None, :]), r_next)
    o = o.reshape(bsz, T, H, d)
    oc = o - jnp.mean(o, axis=-1, keepdims=True)
    o = oc * lax.rsqrt(jnp.mean(oc * oc, axis=-1, keepdims=True) + EPS)
    o = o.reshape(bsz, T, H * d) * g_gn.astype(f32)
    return (o * jax.nn.silu(gate.astype(f32))).astype(gate.dtype)


def memory_cross_attention(h, m, w_q, w_k, w_v, w_o):
    bsz, T, _ = h.shape
    q = (h @ w_q).reshape(bsz, T, X_HEADS, X_HEAD_DIM)
    k = (m @ w_k).reshape(bsz, -1, X_HEADS, X_HEAD_DIM)
    v = (m @ w_v).reshape(bsz, -1, X_HEADS, X_HEAD_DIM)
    s = jnp.einsum('bthd,bmhd->bhtm', q, k, preferred_element_type=jnp.float32) * X_HEAD_DIM ** -0.5
    p = jax.nn.softmax(s, axis=-1).astype(v.dtype)
    o = jnp.einsum('bhtm,bmhd->bthd', p, v).reshape(bsz, T, D_MODEL)
    return o @ w_o


def encoder_trunk(x, mem, g_mix, w_in, a_q_norm, a_k_norm, a_out_norm, b_sink, b_out_norm,
                  c_decay_fwd, c_decay_bwd, c_gn, w_out, g_cross, g_mem, w_xq, w_xk, w_xv, w_xo,
                  g_ffn, w_gate, w_up, w_down, g_final):
    bsz, T, _ = x.shape
    rows = T // GRID_W
    row = jnp.repeat(jnp.arange(rows), GRID_W)
    col = jnp.tile(jnp.arange(GRID_W), rows)
    ang_axial = jnp.concatenate([rope_freqs(row, HEAD_DIM // 2), rope_freqs(col, HEAD_DIM // 2)], axis=-1)
    ang_seq = rope_freqs(jnp.arange(T), HEAD_DIM)
    cuts = [int(c) for c in np.cumsum(IN_SPLITS)[:-1]]

    def heads(t):
        return t.reshape(bsz, T, -1, HEAD_DIM)

    for l in range(DEPTH):
        h = rmsnorm(x, g_mix[l])
        qa, ka, va, qb, kb, vb, qc, kc, vc, gc = jnp.split(h @ w_in[l], cuts, axis=-1)
        oa = rmsnorm(axial_attention(heads(qa), heads(ka), heads(va), a_q_norm[l], a_k_norm[l], ang_axial),
                     a_out_norm[l])
        ob = rmsnorm(window_attention(heads(qb), heads(kb), heads(vb), b_sink[l], ang_seq), b_out_norm[l])
        oc = bidir_retention(heads(qc), heads(kc), heads(vc), gc, c_decay_fwd[l], c_decay_bwd[l], c_gn[l], ang_seq)
        x = x + jnp.concatenate([oa, ob, oc], axis=-1) @ w_out[l]
        x = x + memory_cross_attention(rmsnorm(x, g_cross[l]), rmsnorm(mem, g_mem[l]),
                                       w_xq[l], w_xk[l], w_xv[l], w_xo[l])
        h = rmsnorm(x, g_ffn[l])
        x = x + (jax.nn.silu(h @ w_gate[l]) * (h @ w_up[l])) @ w_down[l]
    return rmsnorm(x, g_final)


def setup_inputs(seed: int = 0) -> dict:
    key = jax.random.key(seed)
    ks = iter(jax.random.split(key, 32))

    def nrm(shape, scale):
        return scale * jax.random.normal(next(ks), shape, jnp.float32)

    def gain(shape):
        return 1.0 + 0.05 * jax.random.normal(next(ks), shape, jnp.float32)

    base = jnp.asarray(np.log(-np.log(1.0 - 2.0 ** (-5.0 - np.arange(C_HEADS)))).astype(np.float32))
    L = DEPTH
    return {
        'x_prompt': nrm((BATCH, SEQ, D_MODEL), 1.0),
        'x_sample': nrm((DEC_BATCH, DEC_SEQ, D_MODEL), 1.0),
        'mem_prompt': nrm((BATCH, MEM_LEN, D_MODEL), 1.0),
        'mem_sample': nrm((DEC_BATCH, MEM_LEN, D_MODEL), 1.0),
        'g_mix': gain((L, D_MODEL)),
        'w_in': nrm((L, D_MODEL, IN_W), D_MODEL ** -0.5),
        'a_q_norm': gain((L, HEAD_DIM)),
        'a_k_norm': gain((L, HEAD_DIM)),
        'a_out_norm': gain((L, A_Q)),
        'b_sink': nrm((L, B_HEADS), 0.5),
        'b_out_norm': gain((L, B_Q)),
        'c_decay_fwd': base[None, :] + nrm((L, C_HEADS), 0.05),
        'c_decay_bwd': base[None, :] + nrm((L, C_HEADS), 0.05),
        'c_gn': gain((L, C_W)),
        'w_out': nrm((L, MIX_W, D_MODEL), MIX_W ** -0.5),
        'g_cross': gain((L, D_MODEL)),
        'g_mem': gain((L, D_MODEL)),
        'w_xq': nrm((L, D_MODEL, D_MODEL), D_MODEL ** -0.5),
        'w_xk': nrm((L, D_MODEL, D_MODEL), D_MODEL ** -0.5),
        'w_xv': nrm((L, D_MODEL, D_MODEL), D_MODEL ** -0.5),
        'w_xo': nrm((L, D_MODEL, D_MODEL), D_MODEL ** -0.5),
        'g_ffn': gain((L, D_MODEL)),
        'w_gate': nrm((L, D_MODEL, D_FF), D_MODEL ** -0.5),
        'w_up': nrm((L, D_MODEL, D_FF), D_MODEL ** -0.5),
        'w_down': nrm((L, D_FF, D_MODEL), D_FF ** -0.5),
        'g_final': gain((D_MODEL,)),
    }


def reference(x_prompt, x_sample, mem_prompt, mem_sample, g_mix, w_in, a_q_norm, a_k_norm, a_out_norm,
              b_sink, b_out_norm, c_decay_fwd, c_decay_bwd, c_gn, w_out, g_cross, g_mem, w_xq, w_xk,
              w_xv, w_xo, g_ffn, w_gate, w_up, w_down, g_final):
    y_prompt = encoder_trunk(x_prompt, mem_prompt, g_mix, w_in, a_q_norm, a_k_norm, a_out_norm, b_sink,
                             b_out_norm, c_decay_fwd, c_decay_bwd, c_gn, w_out, g_cross, g_mem, w_xq,
                             w_xk, w_xv, w_xo, g_ffn, w_gate, w_up, w_down, g_final)
    y_sample = encoder_trunk(x_sample, mem_sample, g_mix, w_in, a_q_norm, a_k_norm, a_out_norm, b_sink,
                             b_out_norm, c_decay_fwd, c_decay_bwd, c_gn, w_out, g_cross, g_mem, w_xq,
                             w_xk, w_xv, w_xo, g_ffn, w_gate, w_up, w_down, g_final)
    return (y_prompt, y_sample)
```

```python
import functools

import jax
import jax.numpy as jnp
from jax import lax
from jax.experimental import pallas as pl
from jax.experimental.pallas import tpu as pltpu

F32 = jnp.float32
BF16 = jnp.bfloat16

D_MODEL = 1024
DEPTH = 4
HEAD_DIM = 64
GRID_W = 64
A_HEADS = 8
B_HEADS = 4
C_HEADS = 4
WINDOW = 128
CHUNK = 128
ROPE_THETA = 10000.0
X_HEADS = 4
X_HEAD_DIM = D_MODEL // X_HEADS
D_FF = 2816
EPS = 1e-6
A_Q = A_HEADS * HEAD_DIM
B_Q = B_HEADS * HEAD_DIM
C_W = C_HEADS * HEAD_DIM
IN_W = 2304
MASK_VALUE = -1e30

LANES = 128
VMEM_LIMIT = 56 * 1024 * 1024

NT_DIMS = (((1,), (1,)), ((), ()))
TN_DIMS = (((0,), (0,)), ((), ()))


def _rms(x, g):
    return (x * lax.rsqrt(jnp.mean(x * x, axis=-1, keepdims=True) + EPS)) * g


def _split_dot(x, m):
    hi = x.astype(BF16)
    lo = (x - hi.astype(F32)).astype(BF16)
    return jnp.dot(hi, m, preferred_element_type=F32) + jnp.dot(lo, m, preferred_element_type=F32)


def _left_lanes(rows):
    return lax.broadcasted_iota(jnp.int32, (rows, LANES), 1) < HEAD_DIM


def _in_proj_kernel(x_ref, g_ref, w_ref, gq_ref, gk_ref, cosa_ref, sina_ref, coss_ref, sins_ref, seg_ref,
                    qa_ref, ka_ref, va_ref, qb_ref, kb_ref, vb_ref, qc_ref, kc_ref, vc_ref, gc_ref):
    tm = x_ref.shape[0]
    h = _rms(x_ref[...], g_ref[...]).astype(BF16)
    lane = lax.broadcasted_iota(jnp.int32, (tm, LANES), 1)
    first_half = (lane & (HEAD_DIM // 2)) == 0
    seg = seg_ref[...]
    cosa, sina = cosa_ref[...], sina_ref[...]
    coss, sins = coss_ref[...], sins_ref[...]

    def proj(c0, width):
        return jnp.dot(h, w_ref[:, c0:c0 + width], preferred_element_type=F32)

    def rope(y, cos, sin_signed):
        partner = jnp.where(first_half, pltpu.roll(y, LANES - HEAD_DIM // 2, 1), pltpu.roll(y, HEAD_DIM // 2, 1))
        return y * cos + partner * sin_signed

    def head_norm(y, g):
        ms = _split_dot(y * y, seg) * (1.0 / HEAD_DIM)
        return (y * lax.rsqrt(ms + EPS)) * g

    def slabs(acc):
        return [acc[:, s * LANES:(s + 1) * LANES] for s in range(acc.shape[1] // LANES)]

    scale = HEAD_DIM ** -0.5
    gq, gk = gq_ref[...], gk_ref[...]
    for s, y in enumerate(slabs(proj(0, 512))):
        qa_ref[:, s * LANES:(s + 1) * LANES] = (rope(head_norm(y, gq), cosa, sina) * scale).astype(BF16)
    ka_ref[...] = rope(head_norm(proj(512, 128), gk), cosa, sina).astype(BF16)
    va_ref[...] = proj(640, 128).astype(BF16)
    for s, y in enumerate(slabs(proj(768, 256))):
        qb_ref[:, s * LANES:(s + 1) * LANES] = (rope(y, coss, sins) * scale).astype(BF16)
    kb_ref[...] = rope(proj(1024, 128), coss, sins).astype(BF16)
    vb_ref[...] = proj(1152, 128).astype(BF16)
    for s, y in enumerate(slabs(proj(1280, 256))):
        qc_ref[:, s * LANES:(s + 1) * LANES] = rope(y, coss, sins).astype(BF16)
    for s, y in enumerate(slabs(proj(1536, 256))):
        kc_ref[:, s * LANES:(s + 1) * LANES] = (rope(y, coss, sins) * scale).astype(BF16)
    vc_ref[...] = proj(1792, 256).astype(BF16)
    gate = proj(2048, 256)
    gc_ref[...] = (gate / (1.0 + jnp.exp(-gate))).astype(BF16)


def _in_proj(x, g, w, gq, gk, tabs, seg, seq_len, tm=512):
    n = x.shape[0]
    nt = seq_len // tm
    row = lambda i: (i, 0)
    const = lambda i: (0, 0)
    tab = lambda i: (i % nt, 0)
    widths = (512, 128, 128, 256, 128, 128, 256, 256, 256, 256)
    return pl.pallas_call(
        _in_proj_kernel,
        grid=(n // tm,),
        in_specs=[pl.BlockSpec((tm, D_MODEL), row), pl.BlockSpec((1, D_MODEL), const),
                  pl.BlockSpec((D_MODEL, IN_W), const), pl.BlockSpec((1, LANES), const),
                  pl.BlockSpec((1, LANES), const)]
                 + [pl.BlockSpec((tm, LANES), tab)] * 4 + [pl.BlockSpec((LANES, LANES), const)],
        out_specs=[pl.BlockSpec((tm, wd), row) for wd in widths],
        out_shape=[jax.ShapeDtypeStruct((n, wd), BF16) for wd in widths],
        compiler_params=pltpu.CompilerParams(dimension_semantics=("parallel",), vmem_limit_bytes=VMEM_LIMIT),
        name="in_proj",
    )(x, g, w, gq, gk, *tabs, seg)


def _attn_a_kernel(q_ref, k_ref, v_ref, g_ref, o_ref, qs_ref, m_ref, l_ref, acc_ref):
    tq = q_ref.shape[0]
    j = pl.program_id(2)
    left = _left_lanes(tq)

    @pl.when(j == 0)
    def _():
        for p in range(A_HEADS // 2):
            slab = q_ref[:, p * LANES:(p + 1) * LANES].astype(F32)
            lo = jnp.where(left, slab, 0.0)
            hi = jnp.where(left, 0.0, slab)
            if p < A_HEADS // 4:
                pair = (lo, pltpu.roll(hi, HEAD_DIM, 1))
            else:
                pair = (pltpu.roll(lo, HEAD_DIM, 1), hi)
            qs_ref[(2 * p) * tq:(2 * p + 1) * tq, :] = pair[0].astype(BF16)
            qs_ref[(2 * p + 1) * tq:(2 * p + 2) * tq, :] = pair[1].astype(BF16)
        m_ref[...] = jnp.full(m_ref.shape, MASK_VALUE, F32)
        l_ref[...] = jnp.zeros(l_ref.shape, F32)
        acc_ref[...] = jnp.zeros(acc_ref.shape, F32)

    k = k_ref[...]
    v = v_ref[...]

    def head_body(hd, carry):
        rows = pl.ds(pl.multiple_of(hd * tq, tq), tq)
        s = lax.dot_general(qs_ref[rows, :], k, NT_DIMS, preferred_element_type=F32)
        m_prev = m_ref[rows, :]
        m_new = jnp.maximum(m_prev, jnp.max(s, axis=1, keepdims=True))
        alpha = jnp.exp(m_prev - m_new)
        p = jnp.exp(s - m_new[:, :1])
        l_ref[rows, :] = alpha * l_ref[rows, :] + jnp.sum(p, axis=1, keepdims=True)
        m_ref[rows, :] = m_new
        acc_ref[rows, :] = alpha * acc_ref[rows, :] + jnp.dot(p.astype(BF16), v, preferred_element_type=F32)
        return carry

    lax.fori_loop(0, A_HEADS, head_body, 0)

    @pl.when(j == pl.num_programs(2) - 1)
    def _():
        out = []
        for p in range(A_HEADS // 2):
            r0 = slice((2 * p) * tq, (2 * p + 1) * tq)
            r1 = slice((2 * p + 1) * tq, (2 * p + 2) * tq)
            a0 = acc_ref[r0, :] / l_ref[r0, :]
            a1 = acc_ref[r1, :] / l_ref[r1, :]
            if p < A_HEADS // 4:
                out.append(jnp.where(left, a0, pltpu.roll(a1, HEAD_DIM, 1)))
            else:
                out.append(jnp.where(left, pltpu.roll(a0, HEAD_DIM, 1), a1))
        o_ref[...] = _rms(jnp.concatenate(out, axis=1), g_ref[...]).astype(BF16)


def _attn_a(q, k, v, g, tq=256, tk=512):
    b, t, _ = q.shape
    tk = min(tk, t)
    rows = A_HEADS * tq
    return pl.pallas_call(
        _attn_a_kernel,
        grid=(b, t // tq, t // tk),
        in_specs=[pl.BlockSpec((None, tq, A_Q), lambda bi, i, j: (bi, i, 0)),
                  pl.BlockSpec((None, tk, LANES), lambda bi, i, j: (bi, j, 0)),
                  pl.BlockSpec((None, tk, LANES), lambda bi, i, j: (bi, j, 0)),
                  pl.BlockSpec((1, A_Q), lambda bi, i, j: (0, 0))],
        out_specs=pl.BlockSpec((None, tq, A_Q), lambda bi, i, j: (bi, i, 0)),
        out_shape=jax.ShapeDtypeStruct((b, t, A_Q), BF16),
        scratch_shapes=[pltpu.VMEM((rows, LANES), BF16), pltpu.VMEM((rows, LANES), F32),
                        pltpu.VMEM((rows, LANES), F32), pltpu.VMEM((rows, LANES), F32)],
        compiler_params=pltpu.CompilerParams(dimension_semantics=("parallel", "parallel", "arbitrary"),
                                             vmem_limit_bytes=VMEM_LIMIT),
        name="attn_a",
    )(q, k, v, g)


def _attn_b_kernel(sink_ref, q_ref, kp_ref, km_ref, kn_ref, vp_ref, vm_ref, vn_ref, g_ref, o_ref,
                   kw_ref, vw_ref, *, seq_len):
    tq = q_ref.shape[0]
    w = WINDOW
    i = pl.program_id(1)
    kw_ref[0:w, :] = kp_ref[...]
    kw_ref[w:w + tq, :] = km_ref[...]
    kw_ref[w + tq:2 * w + tq, :] = kn_ref[...]
    vw_ref[0:w, :] = vp_ref[...]
    vw_ref[w:w + tq, :] = vm_ref[...]
    vw_ref[w + tq:2 * w + tq, :] = vn_ref[...]
    qi = lax.broadcasted_iota(jnp.int32, (w, 3 * w), 0)
    kj = lax.broadcasted_iota(jnp.int32, (w, 3 * w), 1)
    band = jnp.abs(kj - w - qi) <= w
    left = _left_lanes(w)
    for c in range(tq // w):
        kpos = (i * (tq // w) + c - 1) * w + kj
        valid = band & (kpos >= 0) & (kpos < seq_len)
        kwin = kw_ref[c * w:(c + 3) * w, :]
        vwin = vw_ref[c * w:(c + 3) * w, :]
        qf = q_ref[c * w:(c + 1) * w, :].astype(F32)
        s0, s1 = qf[:, :LANES], qf[:, LANES:]
        qh = (jnp.where(left, s0, 0.0), pltpu.roll(jnp.where(left, 0.0, s0), HEAD_DIM, 1),
              pltpu.roll(jnp.where(left, s1, 0.0), HEAD_DIM, 1), jnp.where(left, 0.0, s1))
        outs = []
        for hd in range(B_HEADS):
            s = lax.dot_general(qh[hd].astype(BF16), kwin, NT_DIMS, preferred_element_type=F32)
            s = jnp.where(valid, s, MASK_VALUE)
            sink = sink_ref[hd]
            m = jnp.maximum(jnp.max(s, axis=1, keepdims=True), sink)
            p = jnp.exp(s - m)
            denom = jnp.sum(p, axis=1, keepdims=True) + jnp.exp(sink - m)
            outs.append(jnp.dot(p.astype(BF16), vwin, preferred_element_type=F32) / denom)
        slab0 = jnp.where(left, outs[0], pltpu.roll(outs[1], HEAD_DIM, 1))
        slab1 = jnp.where(left, pltpu.roll(outs[2], HEAD_DIM, 1), outs[3])
        o = _rms(jnp.concatenate([slab0, slab1], axis=1), g_ref[...])
        o_ref[c * w:(c + 1) * w, :] = o.astype(BF16)


def _attn_b(q, k, v, sink, g, tq=512):
    b, t, _ = q.shape
    w = WINDOW
    r = tq // w
    nb = t // w
    main = lambda bi, i: (bi, i, 0)
    prev = lambda bi, i: (bi, jnp.maximum(i * r - 1, 0), 0)
    nxt = lambda bi, i: (bi, jnp.minimum(i * r + r, nb - 1), 0)
    kv_specs = [pl.BlockSpec((None, w, LANES), prev), pl.BlockSpec((None, tq, LANES), main),
                pl.BlockSpec((None, w, LANES), nxt)]
    return pl.pallas_call(
        functools.partial(_attn_b_kernel, seq_len=t),
        grid=(b, t // tq),
        in_specs=[pl.BlockSpec(memory_space=pltpu.SMEM), pl.BlockSpec((None, tq, B_Q), main)]
                 + kv_specs + kv_specs + [pl.BlockSpec((1, B_Q), lambda bi, i: (0, 0))],
        out_specs=pl.BlockSpec((None, tq, B_Q), main),
        out_shape=jax.ShapeDtypeStruct((b, t, B_Q), BF16),
        scratch_shapes=[pltpu.VMEM((tq + 2 * w, LANES), BF16), pltpu.VMEM((tq + 2 * w, LANES), BF16)],
        compiler_params=pltpu.CompilerParams(dimension_semantics=("parallel", "parallel"),
                                             vmem_limit_bytes=VMEM_LIMIT),
        name="attn_b",
    )(sink, q, k, k, k, v, v, v, g)


def _ret_fwd_kernel(q_ref, k_ref, v_ref, mask_ref, dq_ref, dk_ref, dec_ref, bm_ref, o_ref, state_ref, *, cps):
    c_len = CHUNK

    @pl.when(pl.program_id(1) == 0)
    def _():
        state_ref[...] = jnp.zeros(state_ref.shape, F32)

    lane_head = lax.broadcasted_iota(jnp.int32, (c_len, C_W), 1) // HEAD_DIM
    for c in range(cps):
        rows = slice(c * c_len, (c + 1) * c_len)
        q = q_ref[rows, :].astype(F32)
        k = k_ref[rows, :]
        v = v_ref[rows, :]
        o = jnp.dot((q * dq_ref[...]).astype(BF16), state_ref[...].astype(BF16), preferred_element_type=F32)
        for hd in range(C_HEADS):
            sel = lane_head == hd
            qm = jnp.where(sel, q, 0.0).astype(BF16)
            s = lax.dot_general(qm, k, NT_DIMS, preferred_element_type=F32) * mask_ref[hd]
            o = o + jnp.where(sel, jnp.dot(s.astype(BF16), v, preferred_element_type=F32), 0.0)
        o_ref[rows, :] = o
        kd = (k.astype(F32) * dk_ref[...]).astype(BF16)
        kv = lax.dot_general(kd, v, TN_DIMS, preferred_element_type=F32)
        state_ref[...] = state_ref[...] * dec_ref[...] + kv * bm_ref[...]


def _ret_bwd_kernel(q_ref, k_ref, v_ref, part_ref, gate_ref, dq_ref, dk_ref, dec_ref, bm_ref, seg_ref, g_ref,
                    o_ref, state_ref, *, cps):
    c_len = CHUNK

    @pl.when(pl.program_id(1) == 0)
    def _():
        state_ref[...] = jnp.zeros(state_ref.shape, F32)

    seg = seg_ref[...]
    for c in reversed(range(cps)):
        rows = slice(c * c_len, (c + 1) * c_len)
        q = q_ref[rows, :].astype(F32)
        k = k_ref[rows, :]
        v = v_ref[rows, :]
        o = part_ref[rows, :] + jnp.dot((q * dq_ref[...]).astype(BF16), state_ref[...].astype(BF16),
                                        preferred_element_type=F32)
        kd = (k.astype(F32) * dk_ref[...]).astype(BF16)
        kv = lax.dot_general(kd, v, TN_DIMS, preferred_element_type=F32)
        state_ref[...] = state_ref[...] * dec_ref[...] + kv * bm_ref[...]
        oc = o - _split_dot(o, seg) * (1.0 / HEAD_DIM)
        var = _split_dot(oc * oc, seg) * (1.0 / HEAD_DIM)
        on = (oc * lax.rsqrt(var + EPS)) * g_ref[...]
        o_ref[rows, :] = (on * gate_ref[rows, :].astype(F32)).astype(BF16)


def _retention(q, k, v, gate, consts, g_gn, cps=4):
    b, t, _ = q.shape
    mask, dqf, dkf, decf, dqb, dkb, decb, bmask, seg = consts
    tr = cps * CHUNK
    ns = t // tr
    fwd = lambda bi, i: (bi, i, 0)
    bwd = lambda bi, i: (bi, ns - 1 - i, 0)
    c2 = lambda bi, i: (0, 0)
    c3 = lambda bi, i: (0, 0, 0)
    params = pltpu.CompilerParams(dimension_semantics=("parallel", "arbitrary"), vmem_limit_bytes=VMEM_LIMIT)
    part = pl.pallas_call(
        functools.partial(_ret_fwd_kernel, cps=cps),
        grid=(b, ns),
        in_specs=[pl.BlockSpec((None, tr, C_W), fwd)] * 3
                 + [pl.BlockSpec((C_HEADS, CHUNK, CHUNK), c3), pl.BlockSpec((CHUNK, C_W), c2),
                    pl.BlockSpec((CHUNK, C_W), c2), pl.BlockSpec((1, C_W), c2), pl.BlockSpec((C_W, C_W), c2)],
        out_specs=pl.BlockSpec((None, tr, C_W), fwd),
        out_shape=jax.ShapeDtypeStruct((b, t, C_W), F32),
        scratch_shapes=[pltpu.VMEM((C_W, C_W), F32)],
        compiler_params=params,
        name="retention_fwd",
    )(q, k, v, mask, dqf, dkf, decf, bmask)
    return pl.pallas_call(
        functools.partial(_ret_bwd_kernel, cps=cps),
        grid=(b, ns),
        in_specs=[pl.BlockSpec((None, tr, C_W), bwd)] * 5
                 + [pl.BlockSpec((CHUNK, C_W), c2), pl.BlockSpec((CHUNK, C_W), c2), pl.BlockSpec((1, C_W), c2),
                    pl.BlockSpec((C_W, C_W), c2), pl.BlockSpec((C_W, C_W), c2), pl.BlockSpec((1, C_W), c2)],
        out_specs=pl.BlockSpec((None, tr, C_W), bwd),
        out_shape=jax.ShapeDtypeStruct((b, t, C_W), BF16),
        scratch_shapes=[pltpu.VMEM((C_W, C_W), F32)],
        compiler_params=params,
        name="retention_bwd",
    )(q, k, v, part, gate, dqb, dkb, decb, bmask, seg, g_gn)


def _mem_proj_kernel(m_ref, g_ref, wk_ref, wv_ref, k_ref, v_ref):
    h = _rms(m_ref[...], g_ref[...]).astype(BF16)
    k_ref[...] = jnp.dot(h, wk_ref[...], preferred_element_type=F32).astype(BF16)
    v_ref[...] = jnp.dot(h, wv_ref[...], preferred_element_type=F32).astype(BF16)


def _mem_proj(mem, g, wk, wv, tm=256):
    n = mem.shape[0]
    row = lambda i: (i, 0)
    const = lambda i: (0, 0)
    return pl.pallas_call(
        _mem_proj_kernel,
        grid=(n // tm,),
        in_specs=[pl.BlockSpec((tm, D_MODEL), row), pl.BlockSpec((1, D_MODEL), const),
                  pl.BlockSpec((D_MODEL, D_MODEL), const), pl.BlockSpec((D_MODEL, D_MODEL), const)],
        out_specs=[pl.BlockSpec((tm, D_MODEL), row)] * 2,
        out_shape=[jax.ShapeDtypeStruct((n, D_MODEL), BF16)] * 2,
        compiler_params=pltpu.CompilerParams(dimension_semantics=("parallel",), vmem_limit_bytes=VMEM_LIMIT),
        name="mem_proj",
    )(mem, g, wk, wv)


def _mid_kernel(x_ref, oa_ref, ob_ref, oc_ref, wo_ref, g_ref, wq_ref, km_ref, vm_ref, wxo_ref, o_ref):
    x1 = (x_ref[...]
          + jnp.dot(oa_ref[...], wo_ref[0:A_Q, :], preferred_element_type=F32)
          + jnp.dot(ob_ref[...], wo_ref[A_Q:A_Q + B_Q, :], preferred_element_type=F32)
          + jnp.dot(oc_ref[...], wo_ref[A_Q + B_Q:, :], preferred_element_type=F32))
    h = _rms(x1, g_ref[...]).astype(BF16)
    q = (jnp.dot(h, wq_ref[...], preferred_element_type=F32) * (X_HEAD_DIM ** -0.5)).astype(BF16)
    heads = []
    for hd in range(X_HEADS):
        cols = slice(hd * X_HEAD_DIM, (hd + 1) * X_HEAD_DIM)
        s = lax.dot_general(q[:, cols], km_ref[:, cols], NT_DIMS, preferred_element_type=F32)
        p = jnp.exp(s - jnp.max(s, axis=1, keepdims=True))
        o = jnp.dot(p.astype(BF16), vm_ref[:, cols], preferred_element_type=F32)
        heads.append((o / jnp.sum(p, axis=1, keepdims=True)).astype(BF16))
    o_ref[...] = x1 + jnp.dot(jnp.concatenate(heads, axis=1), wxo_ref[...], preferred_element_type=F32)


def _mid(x, oa, ob, oc, wo, g, wq, km, vm, wxo, tm=512):
    b, t, _ = x.shape
    mlen = km.shape[1]
    row = lambda bi, i: (bi, i, 0)
    const = lambda bi, i: (0, 0)
    per_b = lambda bi, i: (bi, 0, 0)
    wspec = pl.BlockSpec((D_MODEL, D_MODEL), const)
    return pl.pallas_call(
        _mid_kernel,
        grid=(b, t // tm),
        in_specs=[pl.BlockSpec((None, tm, D_MODEL), row), pl.BlockSpec((None, tm, A_Q), row),
                  pl.BlockSpec((None, tm, B_Q), row), pl.BlockSpec((None, tm, C_W), row),
                  wspec, pl.BlockSpec((1, D_MODEL), const), wspec,
                  pl.BlockSpec((None, mlen, D_MODEL), per_b), pl.BlockSpec((None, mlen, D_MODEL), per_b), wspec],
        out_specs=pl.BlockSpec((None, tm, D_MODEL), row),
        out_shape=jax.ShapeDtypeStruct((b, t, D_MODEL), F32),
        compiler_params=pltpu.CompilerParams(dimension_semantics=("parallel", "parallel"),
                                             vmem_limit_bytes=VMEM_LIMIT),
        name="mid",
    )(x, oa, ob, oc, wo, g, wq, km, vm, wxo)


def _ffn_kernel(x_ref, g_ref, wg_ref, wu_ref, wd_ref, gf_ref, o_ref, h_ref, *, final_norm):
    j = pl.program_id(1)

    @pl.when(j == 0)
    def _():
        x = x_ref[...]
        h_ref[...] = _rms(x, g_ref[...]).astype(BF16)
        o_ref[...] = x

    h = h_ref[...]
    a = jnp.dot(h, wg_ref[...], preferred_element_type=F32)
    u = jnp.dot(h, wu_ref[...], preferred_element_type=F32)
    t = ((a / (1.0 + jnp.exp(-a))) * u).astype(BF16)
    o_ref[...] += jnp.dot(t, wd_ref[...], preferred_element_type=F32)

    if final_norm:
        @pl.when(j == pl.num_programs(1) - 1)
        def _():
            o_ref[...] = _rms(o_ref[...], gf_ref[...])


def _ffn(x, g, wg, wu, wd, gf, final_norm, tm=512, tf=1408):
    n = x.shape[0]
    row = lambda i, j: (i, 0)
    const = lambda i, j: (0, 0)
    return pl.pallas_call(
        functools.partial(_ffn_kernel, final_norm=final_norm),
        grid=(n // tm, D_FF // tf),
        in_specs=[pl.BlockSpec((tm, D_MODEL), row), pl.BlockSpec((1, D_MODEL), const),
                  pl.BlockSpec((D_MODEL, tf), lambda i, j: (0, j)), pl.BlockSpec((D_MODEL, tf), lambda i, j: (0, j)),
                  pl.BlockSpec((tf, D_MODEL), lambda i, j: (j, 0)), pl.BlockSpec((1, D_MODEL), const)],
        out_specs=pl.BlockSpec((tm, D_MODEL), row),
        out_shape=jax.ShapeDtypeStruct((n, D_MODEL), F32),
        scratch_shapes=[pltpu.VMEM((tm, D_MODEL), BF16)],
        compiler_params=pltpu.CompilerParams(dimension_semantics=("parallel", "arbitrary"),
                                             vmem_limit_bytes=VMEM_LIMIT),
        name="ffn",
    )(x, g, wg, wu, wd, gf)


def _rope_freqs(pos, dim):
    inv = ROPE_THETA ** (-jnp.arange(0, dim, 2, dtype=F32) / dim)
    return pos.astype(F32)[:, None] * inv[None, :]


def _rope_tables(t):
    rows = t // GRID_W
    row = jnp.repeat(jnp.arange(rows), GRID_W)
    col = jnp.tile(jnp.arange(GRID_W), rows)
    ang_axial = jnp.concatenate([_rope_freqs(row, HEAD_DIM // 2), _rope_freqs(col, HEAD_DIM // 2)], axis=-1)
    ang_seq = _rope_freqs(jnp.arange(t), HEAD_DIM)

    def lanes(ang):
        cos, sin = jnp.cos(ang), jnp.sin(ang)
        reps = LANES // HEAD_DIM
        return (jnp.tile(jnp.concatenate([cos, cos], axis=-1), (1, reps)),
                jnp.tile(jnp.concatenate([-sin, sin], axis=-1), (1, reps)))

    return lanes(ang_axial) + lanes(ang_seq)


def _retention_consts(p_f, p_b, bmask, seg):
    c = CHUNK
    lgf = -jnp.exp(p_f.astype(F32))
    lgb = -jnp.exp(p_b.astype(F32))
    idx = jnp.arange(c, dtype=F32)
    diff = idx[:, None] - idx[None, :]
    mask = jnp.where(diff >= 0,
                     jnp.exp(lgf[:, None, None] * jnp.maximum(diff, 0.0)),
                     jnp.exp(lgb[:, None, None] * jnp.maximum(-diff, 0.0)))
    lf = jnp.repeat(lgf, HEAD_DIM)[None, :]
    lb = jnp.repeat(lgb, HEAD_DIM)[None, :]
    dqf = jnp.exp(lf * (idx + 1.0)[:, None])
    dkf = jnp.exp(lf * (c - 1 - idx)[:, None])
    decf = jnp.exp(lf * c)
    dqb = jnp.exp(lb * (c - idx)[:, None])
    dkb = jnp.exp(lb * idx[:, None])
    decb = jnp.exp(lb * c)
    return mask, dqf, dkf, decf, dqb, dkb, decb, bmask, seg


def _block_ones(n):
    r = jnp.arange(n) // HEAD_DIM
    return r[:, None] == r[None, :]


def _trunk(x, mem, weights, layer_consts, seg128, g_final):
    b, t, _ = x.shape
    n = b * t
    tabs = _rope_tables(t)
    mem2 = mem.reshape(b * mem.shape[1], D_MODEL)
    xf = x.reshape(n, D_MODEL)
    for l in range(DEPTH):
        wl = weights[l]
        qa, ka, va, qb, kb, vb, qc, kc, vc, gc = _in_proj(
            xf, wl["g_mix"], wl["w_in"], wl["gq"], wl["gk"], tabs, seg128, t)
        r3 = lambda a: a.reshape(b, t, a.shape[-1])
        oa = _attn_a(r3(qa), r3(ka), r3(va), wl["a_out_norm"])
        ob = _attn_b(r3(qb), r3(kb), r3(vb), wl["b_sink"], wl["b_out_norm"])
        oc = _retention(r3(qc), r3(kc), r3(vc), r3(gc), layer_consts[l], wl["c_gn"])
        km, vm = _mem_proj(mem2, wl["g_mem"], wl["w_xk"], wl["w_xv"])
        km = km.reshape(b, -1, D_MODEL)
        vm = vm.reshape(b, -1, D_MODEL)
        x2 = _mid(xf.reshape(b, t, D_MODEL), oa, ob, oc, wl["w_out"], wl["g_cross"], wl["w_xq"], km, vm,
                  wl["w_xo"])
        xf = _ffn(x2.reshape(n, D_MODEL), wl["g_ffn"], wl["w_gate"], wl["w_up"], wl["w_down"], g_final,
                  final_norm=(l == DEPTH - 1))
    return xf.reshape(b, t, D_MODEL)


def kernel(x_prompt, x_sample, mem_prompt, mem_sample, g_mix, w_in, a_q_norm, a_k_norm, a_out_norm, b_sink,
           b_out_norm, c_decay_fwd, c_decay_bwd, c_gn, w_out, g_cross, g_mem, w_xq, w_xk, w_xv, w_xo, g_ffn,
           w_gate, w_up, w_down, g_final):
    row = lambda a: a.astype(F32).reshape(1, -1)
    reps = LANES // HEAD_DIM
    seg128 = _block_ones(LANES).astype(BF16)
    seg256 = _block_ones(C_W).astype(BF16)
    bmask = _block_ones(C_W).astype(F32)
    weights, layer_consts = [], []
    for l in range(DEPTH):
        weights.append(dict(
            g_mix=row(g_mix[l]), w_in=w_in[l].astype(BF16),
            gq=jnp.tile(row(a_q_norm[l]), (1, reps)), gk=jnp.tile(row(a_k_norm[l]), (1, reps)),
            a_out_norm=row(a_out_norm[l]), b_sink=b_sink[l].astype(F32), b_out_norm=row(b_out_norm[l]),
            c_gn=row(c_gn[l]), w_out=w_out[l].astype(BF16), g_cross=row(g_cross[l]), g_mem=row(g_mem[l]),
            w_xq=w_xq[l].astype(BF16), w_xk=w_xk[l].astype(BF16), w_xv=w_xv[l].astype(BF16),
            w_xo=w_xo[l].astype(BF16), g_ffn=row(g_ffn[l]), w_gate=w_gate[l].astype(BF16),
            w_up=w_up[l].astype(BF16), w_down=w_down[l].astype(BF16)))
        layer_consts.append(_retention_consts(c_decay_fwd[l], c_decay_bwd[l], bmask, seg256))
    gf = row(g_final)
    y_prompt = _trunk(x_prompt, mem_prompt, weights, layer_consts, seg128, gf)
    y_sample = _trunk(x_sample, mem_sample, weights, layer_consts, seg128, gf)
    return (y_prompt, y_sample)
```

```python
import functools

import jax
import jax.numpy as jnp
from jax import lax
from jax.experimental import pallas as pl
from jax.experimental.pallas import tpu as pltpu

F32 = jnp.float32
BF16 = jnp.bfloat16

D_MODEL = 1024
DEPTH = 4
HEAD_DIM = 64
GRID_W = 64
A_HEADS = 8
B_HEADS = 4
C_HEADS = 4
WINDOW = 128
CHUNK = 128
ROPE_THETA = 10000.0
X_HEADS = 4
X_HEAD_DIM = D_MODEL // X_HEADS
D_FF = 2816
EPS = 1e-6
A_Q = A_HEADS * HEAD_DIM
B_Q = B_HEADS * HEAD_DIM
C_W = C_HEADS * HEAD_DIM
IN_W = 2304
MASK_VALUE = -1e30

LOG2_E = 1.4426950408889634

LANES = 128
A_TQ = 256
A_TKS = 512
VMEM_LIMIT = 56 * 1024 * 1024

NT_DIMS = (((1,), (1,)), ((), ()))
TN_DIMS = (((0,), (0,)), ((), ()))


def _rms(x, g):
    return (x * lax.rsqrt(jnp.mean(x * x, axis=-1, keepdims=True) + EPS)) * g


def _split_dot(x, m):
    hi = x.astype(BF16)
    lo = (x - hi.astype(F32)).astype(BF16)
    return jnp.dot(hi, m, preferred_element_type=F32) + jnp.dot(lo, m, preferred_element_type=F32)


def _left_lanes(rows):
    return lax.broadcasted_iota(jnp.int32, (rows, LANES), 1) < HEAD_DIM


def _in_proj_kernel(x_ref, g_ref, w_ref, gq_ref, gk_ref, cosa_ref, sina_ref, coss_ref, sins_ref, seg_ref,
                    qa_ref, ka_ref, va_ref, qb_ref, kb_ref, vb_ref, qc_ref, kc_ref, vc_ref, gc_ref):
    tm = x_ref.shape[0]
    h = _rms(x_ref[...], g_ref[...]).astype(BF16)
    lane = lax.broadcasted_iota(jnp.int32, (tm, LANES), 1)
    first_half = (lane & (HEAD_DIM // 2)) == 0
    seg = seg_ref[...]
    cosa, sina = cosa_ref[...], sina_ref[...]
    coss, sins = coss_ref[...], sins_ref[...]

    def proj(c0, width):
        return jnp.dot(h, w_ref[:, c0:c0 + width], preferred_element_type=F32)

    def rope(y, cos, sin_signed):
        partner = jnp.where(first_half, pltpu.roll(y, LANES - HEAD_DIM // 2, 1), pltpu.roll(y, HEAD_DIM // 2, 1))
        return y * cos + partner * sin_signed

    def head_norm(y, g):
        ms = _split_dot(y * y, seg) * (1.0 / HEAD_DIM)
        return (y * lax.rsqrt(ms + EPS)) * g

    def slabs(acc):
        return [acc[:, s * LANES:(s + 1) * LANES] for s in range(acc.shape[1] // LANES)]

    scale = HEAD_DIM ** -0.5
    gq, gk = gq_ref[...], gk_ref[...]
    n_qt = qa_ref.shape[0]
    tq = qa_ref.shape[2]
    for s, y in enumerate(slabs(proj(0, 512))):
        yt = (rope(head_norm(y, gq), cosa, sina) * (scale * LOG2_E)).T.astype(BF16)
        for t in range(n_qt):
            qa_ref[t, s * LANES:(s + 1) * LANES, :] = yt[:, t * tq:(t + 1) * tq]
    ka_ref[...] = rope(head_norm(proj(512, 128), gk), cosa, sina).astype(BF16)
    va_ref[0] = proj(640, 128).T.astype(BF16)
    for s, y in enumerate(slabs(proj(768, 256))):
        qb_ref[:, s * LANES:(s + 1) * LANES] = (rope(y, coss, sins) * scale).astype(BF16)
    kb_ref[...] = rope(proj(1024, 128), coss, sins).astype(BF16)
    vb_ref[...] = proj(1152, 128).astype(BF16)
    for s, y in enumerate(slabs(proj(1280, 256))):
        qc_ref[:, s * LANES:(s + 1) * LANES] = rope(y, coss, sins).astype(BF16)
    for s, y in enumerate(slabs(proj(1536, 256))):
        kc_ref[:, s * LANES:(s + 1) * LANES] = (rope(y, coss, sins) * scale).astype(BF16)
    vc_ref[...] = proj(1792, 256).astype(BF16)
    gate = proj(2048, 256)
    gc_ref[...] = (gate / (1.0 + jnp.exp(-gate))).astype(BF16)


def _in_proj(x, g, w, gq, gk, tabs, seg, seq_len, tm=A_TKS):
    n = x.shape[0]
    nt = seq_len // tm
    row = lambda i: (i, 0)
    const = lambda i: (0, 0)
    tab = lambda i: (i % nt, 0)
    widths = (128, 256, 128, 128, 256, 256, 256, 256)
    lead = lambda i: (i, 0, 0)
    out_specs = ([pl.BlockSpec((tm // A_TQ, A_Q, A_TQ), lead), pl.BlockSpec((tm, LANES), row),
                  pl.BlockSpec((1, LANES, tm), lead)] + [pl.BlockSpec((tm, wd), row) for wd in widths[1:]])
    out_shape = ([jax.ShapeDtypeStruct((n // A_TQ, A_Q, A_TQ), BF16), jax.ShapeDtypeStruct((n, LANES), BF16),
                  jax.ShapeDtypeStruct((n // tm, LANES, tm), BF16)]
                 + [jax.ShapeDtypeStruct((n, wd), BF16) for wd in widths[1:]])
    return pl.pallas_call(
        _in_proj_kernel,
        grid=(n // tm,),
        in_specs=[pl.BlockSpec((tm, D_MODEL), row), pl.BlockSpec((1, D_MODEL), const),
                  pl.BlockSpec((D_MODEL, IN_W), const), pl.BlockSpec((1, LANES), const),
                  pl.BlockSpec((1, LANES), const)]
                 + [pl.BlockSpec((tm, LANES), tab)] * 4 + [pl.BlockSpec((LANES, LANES), const)],
        out_specs=out_specs,
        out_shape=out_shape,
        compiler_params=pltpu.CompilerParams(dimension_semantics=("parallel",), vmem_limit_bytes=VMEM_LIMIT),
        name="in_proj",
    )(x, g, w, gq, gk, *tabs, seg)


def _attn_a_kernel(q_ref, k_ref, v_ref, g_ref, o_ref, qpad_ref, m_ref, l_ref, acc_ref):
    tq = q_ref.shape[1]
    n_sub, _, tks = v_ref.shape
    hd = HEAD_DIM
    heads_per_kv = A_HEADS // 2

    zeros = jnp.zeros((hd, tq), BF16)
    for h in range(A_HEADS):
        qh = q_ref[h * hd:(h + 1) * hd, :]
        qpad_ref[h] = jnp.concatenate([qh, zeros] if h < heads_per_kv else [zeros, qh], axis=0)
    m_ref[...] = jnp.full(m_ref.shape, MASK_VALUE, F32)
    l_ref[...] = jnp.zeros(l_ref.shape, F32)
    acc_ref[...] = jnp.zeros(acc_ref.shape, F32)

    def key_step(c, carry):
        kt = k_ref[pl.ds(pl.multiple_of(c * tks, tks), tks), :]
        vt = v_ref[c]

        def scores(h):
            return jnp.dot(kt, qpad_ref[h], preferred_element_type=F32)

        def softmax(h, s):
            m_prev = m_ref[h:h + 1, :]
            m_new = jnp.maximum(m_prev, jnp.max(s, axis=0, keepdims=True))
            alpha = jnp.exp2(m_prev - m_new)
            p = jnp.exp2(s - m_new)
            l_ref[h:h + 1, :] = alpha * l_ref[h:h + 1, :] + jnp.sum(p, axis=0, keepdims=True)
            m_ref[h:h + 1, :] = m_new
            return p.astype(BF16), alpha

        def accumulate(h, p, alpha):
            kv = h // heads_per_kv
            rows = slice(h * hd, (h + 1) * hd)
            pv = jnp.dot(vt[kv * hd:(kv + 1) * hd, :], p, preferred_element_type=F32)
            acc_ref[rows, :] = alpha * acc_ref[rows, :] + pv

        s = scores(0)
        pending = None
        for h in range(A_HEADS):
            s_next = scores(h + 1) if h + 1 < A_HEADS else None
            p, alpha = softmax(h, s)
            if pending is not None:
                accumulate(*pending)
            pending = (h, p, alpha)
            s = s_next
        accumulate(*pending)
        return carry

    lax.fori_loop(0, n_sub, key_step, 0)

    out_t = jnp.concatenate(
        [acc_ref[h * hd:(h + 1) * hd, :] / l_ref[h:h + 1, :] for h in range(A_HEADS)], axis=0)
    ms = jnp.mean(out_t * out_t, axis=0, keepdims=True)
    out_t = (out_t * lax.rsqrt(ms + EPS)) * g_ref[...]
    o_ref[...] = out_t.T.astype(BF16)


def _attn_a(q_t, k, v_t, g_col, b, t):
    tq, tks = A_TQ, A_TKS
    nq, ns = t // tq, t // tks
    return pl.pallas_call(
        _attn_a_kernel,
        grid=(b, nq),
        in_specs=[pl.BlockSpec((None, A_Q, tq), lambda bi, i: (bi * nq + i, 0, 0)),
                  pl.BlockSpec((None, t, LANES), lambda bi, i: (bi, 0, 0)),
                  pl.BlockSpec((ns, LANES, tks), lambda bi, i: (bi, 0, 0)),
                  pl.BlockSpec((A_Q, 1), lambda bi, i: (0, 0))],
        out_specs=pl.BlockSpec((None, tq, A_Q), lambda bi, i: (bi, i, 0)),
        out_shape=jax.ShapeDtypeStruct((b, t, A_Q), BF16),
        scratch_shapes=[pltpu.VMEM((A_HEADS, 2 * HEAD_DIM, tq), BF16), pltpu.VMEM((A_HEADS, tq), F32),
                        pltpu.VMEM((A_HEADS, tq), F32), pltpu.VMEM((A_Q, tq), F32)],
        compiler_params=pltpu.CompilerParams(dimension_semantics=("parallel", "parallel"),
                                             vmem_limit_bytes=VMEM_LIMIT),
        name="attn_a",
    )(q_t, k, v_t, g_col)


def _attn_b_kernel(sink_ref, q_ref, kp_ref, km_ref, kn_ref, vp_ref, vm_ref, vn_ref, g_ref, o_ref,
                   kw_ref, vw_ref, *, seq_len):
    tq = q_ref.shape[0]
    w = WINDOW
    i = pl.program_id(1)
    kw_ref[0:w, :] = kp_ref[...]
    kw_ref[w:w + tq, :] = km_ref[...]
    kw_ref[w + tq:2 * w + tq, :] = kn_ref[...]
    vw_ref[0:w, :] = vp_ref[...]
    vw_ref[w:w + tq, :] = vm_ref[...]
    vw_ref[w + tq:2 * w + tq, :] = vn_ref[...]
    qi = lax.broadcasted_iota(jnp.int32, (w, 3 * w), 0)
    kj = lax.broadcasted_iota(jnp.int32, (w, 3 * w), 1)
    band = jnp.abs(kj - w - qi) <= w
    left = _left_lanes(w)
    for c in range(tq // w):
        kpos = (i * (tq // w) + c - 1) * w + kj
        valid = band & (kpos >= 0) & (kpos < seq_len)
        kwin = kw_ref[c * w:(c + 3) * w, :]
        vwin = vw_ref[c * w:(c + 3) * w, :]
        qf = q_ref[c * w:(c + 1) * w, :].astype(F32)
        s0, s1 = qf[:, :LANES], qf[:, LANES:]
        qh = (jnp.where(left, s0, 0.0), pltpu.roll(jnp.where(left, 0.0, s0), HEAD_DIM, 1),
              pltpu.roll(jnp.where(left, s1, 0.0), HEAD_DIM, 1), jnp.where(left, 0.0, s1))
        outs = []
        for hd in range(B_HEADS):
            s = lax.dot_general(qh[hd].astype(BF16), kwin, NT_DIMS, preferred_element_type=F32)
            s = jnp.where(valid, s, MASK_VALUE)
            sink = sink_ref[hd]
            m = jnp.maximum(jnp.max(s, axis=1, keepdims=True), sink)
            p = jnp.exp(s - m)
            denom = jnp.sum(p, axis=1, keepdims=True) + jnp.exp(sink - m)
            outs.append(jnp.dot(p.astype(BF16), vwin, preferred_element_type=F32) / denom)
        slab0 = jnp.where(left, outs[0], pltpu.roll(outs[1], HEAD_DIM, 1))
        slab1 = jnp.where(left, pltpu.roll(outs[2], HEAD_DIM, 1), outs[3])
        o = _rms(jnp.concatenate([slab0, slab1], axis=1), g_ref[...])
        o_ref[c * w:(c + 1) * w, :] = o.astype(BF16)


def _attn_b(q, k, v, sink, g, tq=512):
    b, t, _ = q.shape
    w = WINDOW
    r = tq // w
    nb = t // w
    main = lambda bi, i: (bi, i, 0)
    prev = lambda bi, i: (bi, jnp.maximum(i * r - 1, 0), 0)
    nxt = lambda bi, i: (bi, jnp.minimum(i * r + r, nb - 1), 0)
    kv_specs = [pl.BlockSpec((None, w, LANES), prev), pl.BlockSpec((None, tq, LANES), main),
                pl.BlockSpec((None, w, LANES), nxt)]
    return pl.pallas_call(
        functools.partial(_attn_b_kernel, seq_len=t),
        grid=(b, t // tq),
        in_specs=[pl.BlockSpec(memory_space=pltpu.SMEM), pl.BlockSpec((None, tq, B_Q), main)]
                 + kv_specs + kv_specs + [pl.BlockSpec((1, B_Q), lambda bi, i: (0, 0))],
        out_specs=pl.BlockSpec((None, tq, B_Q), main),
        out_shape=jax.ShapeDtypeStruct((b, t, B_Q), BF16),
        scratch_shapes=[pltpu.VMEM((tq + 2 * w, LANES), BF16), pltpu.VMEM((tq + 2 * w, LANES), BF16)],
        compiler_params=pltpu.CompilerParams(dimension_semantics=("parallel", "parallel"),
                                             vmem_limit_bytes=VMEM_LIMIT),
        name="attn_b",
    )(sink, q, k, k, k, v, v, v, g)


def _ret_fwd_kernel(q_ref, k_ref, v_ref, mask_ref, dq_ref, dk_ref, dec_ref, bm_ref, o_ref, state_ref, *, cps):
    c_len = CHUNK

    @pl.when(pl.program_id(1) == 0)
    def _():
        state_ref[...] = jnp.zeros(state_ref.shape, F32)

    lane_head = lax.broadcasted_iota(jnp.int32, (c_len, C_W), 1) // HEAD_DIM
    for c in range(cps):
        rows = slice(c * c_len, (c + 1) * c_len)
        q = q_ref[rows, :].astype(F32)
        k = k_ref[rows, :]
        v = v_ref[rows, :]
        o = jnp.dot((q * dq_ref[...]).astype(BF16), state_ref[...].astype(BF16), preferred_element_type=F32)
        for hd in range(C_HEADS):
            sel = lane_head == hd
            qm = jnp.where(sel, q, 0.0).astype(BF16)
            s = lax.dot_general(qm, k, NT_DIMS, preferred_element_type=F32) * mask_ref[hd]
            o = o + jnp.where(sel, jnp.dot(s.astype(BF16), v, preferred_element_type=F32), 0.0)
        o_ref[rows, :] = o
        kd = (k.astype(F32) * dk_ref[...]).astype(BF16)
        kv = lax.dot_general(kd, v, TN_DIMS, preferred_element_type=F32)
        state_ref[...] = state_ref[...] * dec_ref[...] + kv * bm_ref[...]


def _ret_bwd_kernel(q_ref, k_ref, v_ref, part_ref, gate_ref, dq_ref, dk_ref, dec_ref, bm_ref, seg_ref, g_ref,
                    o_ref, state_ref, *, cps):
    c_len = CHUNK

    @pl.when(pl.program_id(1) == 0)
    def _():
        state_ref[...] = jnp.zeros(state_ref.shape, F32)

    seg = seg_ref[...]
    for c in reversed(range(cps)):
        rows = slice(c * c_len, (c + 1) * c_len)
        q = q_ref[rows, :].astype(F32)
        k = k_ref[rows, :]
        v = v_ref[rows, :]
        o = part_ref[rows, :] + jnp.dot((q * dq_ref[...]).astype(BF16), state_ref[...].astype(BF16),
                                        preferred_element_type=F32)
        kd = (k.astype(F32) * dk_ref[...]).astype(BF16)
        kv = lax.dot_general(kd, v, TN_DIMS, preferred_element_type=F32)
        state_ref[...] = state_ref[...] * dec_ref[...] + kv * bm_ref[...]
        oc = o - _split_dot(o, seg) * (1.0 / HEAD_DIM)
        var = _split_dot(oc * oc, seg) * (1.0 / HEAD_DIM)
        on = (oc * lax.rsqrt(var + EPS)) * g_ref[...]
        o_ref[rows, :] = (on * gate_ref[rows, :].astype(F32)).astype(BF16)


def _retention(q, k, v, gate, consts, g_gn, cps=4):
    b, t, _ = q.shape
    mask, dqf, dkf, decf, dqb, dkb, decb, bmask, seg = consts
    tr = cps * CHUNK
    ns = t // tr
    fwd = lambda bi, i: (bi, i, 0)
    bwd = lambda bi, i: (bi, ns - 1 - i, 0)
    c2 = lambda bi, i: (0, 0)
    c3 = lambda bi, i: (0, 0, 0)
    params = pltpu.CompilerParams(dimension_semantics=("parallel", "arbitrary"), vmem_limit_bytes=VMEM_LIMIT)
    part = pl.pallas_call(
        functools.partial(_ret_fwd_kernel, cps=cps),
        grid=(b, ns),
        in_specs=[pl.BlockSpec((None, tr, C_W), fwd)] * 3
                 + [pl.BlockSpec((C_HEADS, CHUNK, CHUNK), c3), pl.BlockSpec((CHUNK, C_W), c2),
                    pl.BlockSpec((CHUNK, C_W), c2), pl.BlockSpec((1, C_W), c2), pl.BlockSpec((C_W, C_W), c2)],
        out_specs=pl.BlockSpec((None, tr, C_W), fwd),
        out_shape=jax.ShapeDtypeStruct((b, t, C_W), F32),
        scratch_shapes=[pltpu.VMEM((C_W, C_W), F32)],
        compiler_params=params,
        name="retention_fwd",
    )(q, k, v, mask, dqf, dkf, decf, bmask)
    return pl.pallas_call(
        functools.partial(_ret_bwd_kernel, cps=cps),
        grid=(b, ns),
        in_specs=[pl.BlockSpec((None, tr, C_W), bwd)] * 5
                 + [pl.BlockSpec((CHUNK, C_W), c2), pl.BlockSpec((CHUNK, C_W), c2), pl.BlockSpec((1, C_W), c2),
                    pl.BlockSpec((C_W, C_W), c2), pl.BlockSpec((C_W, C_W), c2), pl.BlockSpec((1, C_W), c2)],
        out_specs=pl.BlockSpec((None, tr, C_W), bwd),
        out_shape=jax.ShapeDtypeStruct((b, t, C_W), BF16),
        scratch_shapes=[pltpu.VMEM((C_W, C_W), F32)],
        compiler_params=params,
        name="retention_bwd",
    )(q, k, v, part, gate, dqb, dkb, decb, bmask, seg, g_gn)


def _mem_proj_kernel(m_ref, g_ref, wk_ref, wv_ref, k_ref, v_ref):
    h = _rms(m_ref[...], g_ref[...]).astype(BF16)
    k_ref[...] = jnp.dot(h, wk_ref[...], preferred_element_type=F32).astype(BF16)
    v_ref[...] = jnp.dot(h, wv_ref[...], preferred_element_type=F32).astype(BF16)


def _mem_proj(mem, g, wk, wv, tm=256):
    n = mem.shape[0]
    row = lambda i: (i, 0)
    const = lambda i: (0, 0)
    return pl.pallas_call(
        _mem_proj_kernel,
        grid=(n // tm,),
        in_specs=[pl.BlockSpec((tm, D_MODEL), row), pl.BlockSpec((1, D_MODEL), const),
                  pl.BlockSpec((D_MODEL, D_MODEL), const), pl.BlockSpec((D_MODEL, D_MODEL), const)],
        out_specs=[pl.BlockSpec((tm, D_MODEL), row)] * 2,
        out_shape=[jax.ShapeDtypeStruct((n, D_MODEL), BF16)] * 2,
        compiler_params=pltpu.CompilerParams(dimension_semantics=("parallel",), vmem_limit_bytes=VMEM_LIMIT),
        name="mem_proj",
    )(mem, g, wk, wv)


def _mid_kernel(x_ref, oa_ref, ob_ref, oc_ref, wo_ref, g_ref, wq_ref, km_ref, vm_ref, wxo_ref, o_ref):
    x1 = (x_ref[...]
          + jnp.dot(oa_ref[...], wo_ref[0:A_Q, :], preferred_element_type=F32)
          + jnp.dot(ob_ref[...], wo_ref[A_Q:A_Q + B_Q, :], preferred_element_type=F32)
          + jnp.dot(oc_ref[...], wo_ref[A_Q + B_Q:, :], preferred_element_type=F32))
    h = _rms(x1, g_ref[...]).astype(BF16)
    q = (jnp.dot(h, wq_ref[...], preferred_element_type=F32) * (X_HEAD_DIM ** -0.5)).astype(BF16)
    heads = []
    for hd in range(X_HEADS):
        cols = slice(hd * X_HEAD_DIM, (hd + 1) * X_HEAD_DIM)
        s = lax.dot_general(q[:, cols], km_ref[:, cols], NT_DIMS, preferred_element_type=F32)
        p = jnp.exp(s - jnp.max(s, axis=1, keepdims=True))
        o = jnp.dot(p.astype(BF16), vm_ref[:, cols], preferred_element_type=F32)
        heads.append((o / jnp.sum(p, axis=1, keepdims=True)).astype(BF16))
    o_ref[...] = x1 + jnp.dot(jnp.concatenate(heads, axis=1), wxo_ref[...], preferred_element_type=F32)


def _mid(x, oa, ob, oc, wo, g, wq, km, vm, wxo, tm=512):
    b, t, _ = x.shape
    mlen = km.shape[1]
    row = lambda bi, i: (bi, i, 0)
    const = lambda bi, i: (0, 0)
    per_b = lambda bi, i: (bi, 0, 0)
    wspec = pl.BlockSpec((D_MODEL, D_MODEL), const)
    return pl.pallas_call(
        _mid_kernel,
        grid=(b, t // tm),
        in_specs=[pl.BlockSpec((None, tm, D_MODEL), row), pl.BlockSpec((None, tm, A_Q), row),
                  pl.BlockSpec((None, tm, B_Q), row), pl.BlockSpec((None, tm, C_W), row),
                  wspec, pl.BlockSpec((1, D_MODEL), const), wspec,
                  pl.BlockSpec((None, mlen, D_MODEL), per_b), pl.BlockSpec((None, mlen, D_MODEL), per_b), wspec],
        out_specs=pl.BlockSpec((None, tm, D_MODEL), row),
        out_shape=jax.ShapeDtypeStruct((b, t, D_MODEL), F32),
        compiler_params=pltpu.CompilerParams(dimension_semantics=("parallel", "parallel"),
                                             vmem_limit_bytes=VMEM_LIMIT),
        name="mid",
    )(x, oa, ob, oc, wo, g, wq, km, vm, wxo)


def _ffn_kernel(x_ref, g_ref, wg_ref, wu_ref, wd_ref, gf_ref, o_ref, h_ref, *, final_norm):
    j = pl.program_id(1)

    @pl.when(j == 0)
    def _():
        x = x_ref[...]
        h_ref[...] = _rms(x, g_ref[...]).astype(BF16)
        o_ref[...] = x

    h = h_ref[...]
    a = jnp.dot(h, wg_ref[...], preferred_element_type=F32)
    u = jnp.dot(h, wu_ref[...], preferred_element_type=F32)
    t = ((a / (1.0 + jnp.exp(-a))) * u).astype(BF16)
    o_ref[...] += jnp.dot(t, wd_ref[...], preferred_element_type=F32)

    if final_norm:
        @pl.when(j == pl.num_programs(1) - 1)
        def _():
            o_ref[...] = _rms(o_ref[...], gf_ref[...])


def _ffn(x, g, wg, wu, wd, gf, final_norm, tm=512, tf=1408):
    n = x.shape[0]
    row = lambda i, j: (i, 0)
    const = lambda i, j: (0, 0)
    return pl.pallas_call(
        functools.partial(_ffn_kernel, final_norm=final_norm),
        grid=(n // tm, D_FF // tf),
        in_specs=[pl.BlockSpec((tm, D_MODEL), row), pl.BlockSpec((1, D_MODEL), const),
                  pl.BlockSpec((D_MODEL, tf), lambda i, j: (0, j)), pl.BlockSpec((D_MODEL, tf), lambda i, j: (0, j)),
                  pl.BlockSpec((tf, D_MODEL), lambda i, j: (j, 0)), pl.BlockSpec((1, D_MODEL), const)],
        out_specs=pl.BlockSpec((tm, D_MODEL), row),
        out_shape=jax.ShapeDtypeStruct((n, D_MODEL), F32),
        scratch_shapes=[pltpu.VMEM((tm, D_MODEL), BF16)],
        compiler_params=pltpu.CompilerParams(dimension_semantics=("parallel", "arbitrary"),
                                             vmem_limit_bytes=VMEM_LIMIT),
        name="ffn",
    )(x, g, wg, wu, wd, gf)


def _rope_freqs(pos, dim):
    inv = ROPE_THETA ** (-jnp.arange(0, dim, 2, dtype=F32) / dim)
    return pos.astype(F32)[:, None] * inv[None, :]


def _rope_tables(t):
    rows = t // GRID_W
    row = jnp.repeat(jnp.arange(rows), GRID_W)
    col = jnp.tile(jnp.arange(GRID_W), rows)
    ang_axial = jnp.concatenate([_rope_freqs(row, HEAD_DIM // 2), _rope_freqs(col, HEAD_DIM // 2)], axis=-1)
    ang_seq = _rope_freqs(jnp.arange(t), HEAD_DIM)

    def lanes(ang):
        cos, sin = jnp.cos(ang), jnp.sin(ang)
        reps = LANES // HEAD_DIM
        return (jnp.tile(jnp.concatenate([cos, cos], axis=-1), (1, reps)),
                jnp.tile(jnp.concatenate([-sin, sin], axis=-1), (1, reps)))

    return lanes(ang_axial) + lanes(ang_seq)


def _retention_consts(p_f, p_b, bmask, seg):
    c = CHUNK
    lgf = -jnp.exp(p_f.astype(F32))
    lgb = -jnp.exp(p_b.astype(F32))
    idx = jnp.arange(c, dtype=F32)
    diff = idx[:, None] - idx[None, :]
    mask = jnp.where(diff >= 0,
                     jnp.exp(lgf[:, None, None] * jnp.maximum(diff, 0.0)),
                     jnp.exp(lgb[:, None, None] * jnp.maximum(-diff, 0.0)))
    lf = jnp.repeat(lgf, HEAD_DIM)[None, :]
    lb = jnp.repeat(lgb, HEAD_DIM)[None, :]
    dqf = jnp.exp(lf * (idx + 1.0)[:, None])
    dkf = jnp.exp(lf * (c - 1 - idx)[:, None])
    decf = jnp.exp(lf * c)
    dqb = jnp.exp(lb * (c - idx)[:, None])
    dkb = jnp.exp(lb * idx[:, None])
    decb = jnp.exp(lb * c)
    return mask, dqf, dkf, decf, dqb, dkb, decb, bmask, seg


def _block_ones(n):
    r = jnp.arange(n) // HEAD_DIM
    return r[:, None] == r[None, :]


def _trunk(x, mem, weights, layer_consts, seg128, g_final):
    b, t, _ = x.shape
    n = b * t
    tabs = _rope_tables(t)
    mem2 = mem.reshape(b * mem.shape[1], D_MODEL)
    xf = x.reshape(n, D_MODEL)
    for l in range(DEPTH):
        wl = weights[l]
        qa, ka, va, qb, kb, vb, qc, kc, vc, gc = _in_proj(
            xf, wl["g_mix"], wl["w_in"], wl["gq"], wl["gk"], tabs, seg128, t)
        r3 = lambda a: a.reshape(b, t, a.shape[-1])
        oa = _attn_a(qa, r3(ka), va, wl["a_out_norm"], b, t)
        ob = _attn_b(r3(qb), r3(kb), r3(vb), wl["b_sink"], wl["b_out_norm"])
        oc = _retention(r3(qc), r3(kc), r3(vc), r3(gc), layer_consts[l], wl["c_gn"])
        km, vm = _mem_proj(mem2, wl["g_mem"], wl["w_xk"], wl["w_xv"])
        km = km.reshape(b, -1, D_MODEL)
        vm = vm.reshape(b, -1, D_MODEL)
        x2 = _mid(xf.reshape(b, t, D_MODEL), oa, ob, oc, wl["w_out"], wl["g_cross"], wl["w_xq"], km, vm,
                  wl["w_xo"])
        xf = _ffn(x2.reshape(n, D_MODEL), wl["g_ffn"], wl["w_gate"], wl["w_up"], wl["w_down"], g_final,
                  final_norm=(l == DEPTH - 1))
    return xf.reshape(b, t, D_MODEL)


def kernel(x_prompt, x_sample, mem_prompt, mem_sample, g_mix, w_in, a_q_norm, a_k_norm, a_out_norm, b_sink,
           b_out_norm, c_decay_fwd, c_decay_bwd, c_gn, w_out, g_cross, g_mem, w_xq, w_xk, w_xv, w_xo, g_ffn,
           w_gate, w_up, w_down, g_final):
    row = lambda a: a.astype(F32).reshape(1, -1)
    reps = LANES // HEAD_DIM
    seg128 = _block_ones(LANES).astype(BF16)
    seg256 = _block_ones(C_W).astype(BF16)
    bmask = _block_ones(C_W).astype(F32)
    weights, layer_consts = [], []
    for l in range(DEPTH):
        weights.append(dict(
            g_mix=row(g_mix[l]), w_in=w_in[l].astype(BF16),
            gq=jnp.tile(row(a_q_norm[l]), (1, reps)), gk=jnp.tile(row(a_k_norm[l]), (1, reps)),
            a_out_norm=a_out_norm[l].astype(F32).reshape(-1, 1), b_sink=b_sink[l].astype(F32), b_out_norm=row(b_out_norm[l]),
            c_gn=row(c_gn[l]), w_out=w_out[l].astype(BF16), g_cross=row(g_cross[l]), g_mem=row(g_mem[l]),
            w_xq=w_xq[l].astype(BF16), w_xk=w_xk[l].astype(BF16), w_xv=w_xv[l].astype(BF16),
            w_xo=w_xo[l].astype(BF16), g_ffn=row(g_ffn[l]), w_gate=w_gate[l].astype(BF16),
            w_up=w_up[l].astype(BF16), w_down=w_down[l].astype(BF16)))
        layer_consts.append(_retention_consts(c_decay_fwd[l], c_decay_bwd[l], bmask, seg256))
    gf = row(g_final)
    y_prompt = _trunk(x_prompt, mem_prompt, weights, layer_consts, seg128, gf)
    y_sample = _trunk(x_sample, mem_sample, weights, layer_consts, seg128, gf)
    return (y_prompt, y_sample)
```

```python
import functools

import jax
import jax.numpy as jnp
from jax import lax
from jax.experimental import pallas as pl
from jax.experimental.pallas import tpu as pltpu

F32 = jnp.float32
BF16 = jnp.bfloat16

D_MODEL = 1024
DEPTH = 4
HEAD_DIM = 64
GRID_W = 64
A_HEADS = 8
B_HEADS = 4
C_HEADS = 4
WINDOW = 128
CHUNK = 128
ROPE_THETA = 10000.0
X_HEADS = 4
X_HEAD_DIM = D_MODEL // X_HEADS
D_FF = 2816
EPS = 1e-6
A_Q = A_HEADS * HEAD_DIM
B_Q = B_HEADS * HEAD_DIM
C_W = C_HEADS * HEAD_DIM
IN_W = 2304
MASK_VALUE = -1e30

LOG2_E = 1.4426950408889634

LANES = 128
A_TQ = 256
A_TKS = 512
A_TKI = 1024
VMEM_LIMIT = 56 * 1024 * 1024

NT_DIMS = (((1,), (1,)), ((), ()))
TN_DIMS = (((0,), (0,)), ((), ()))


def _rms(x, g):
    return (x * lax.rsqrt(jnp.mean(x * x, axis=-1, keepdims=True) + EPS)) * g


def _split_dot(x, m):
    hi = x.astype(BF16)
    lo = (x - hi.astype(F32)).astype(BF16)
    return jnp.dot(hi, m, preferred_element_type=F32) + jnp.dot(lo, m, preferred_element_type=F32)


def _left_lanes(rows):
    return lax.broadcasted_iota(jnp.int32, (rows, LANES), 1) < HEAD_DIM


def _in_proj_kernel(x_ref, g_ref, w_ref, gq_ref, gk_ref, cosa_ref, sina_ref, coss_ref, sins_ref, seg_ref,
                    qa_ref, ka_ref, va_ref, qb_ref, kb_ref, vb_ref, qc_ref, kc_ref, vc_ref, gc_ref):
    tm = x_ref.shape[0]
    h = _rms(x_ref[...], g_ref[...]).astype(BF16)
    lane = lax.broadcasted_iota(jnp.int32, (tm, LANES), 1)
    first_half = (lane & (HEAD_DIM // 2)) == 0
    seg = seg_ref[...]
    cosa, sina = cosa_ref[...], sina_ref[...]
    coss, sins = coss_ref[...], sins_ref[...]

    def proj(c0, width):
        return jnp.dot(h, w_ref[:, c0:c0 + width], preferred_element_type=F32)

    def rope(y, cos, sin_signed):
        partner = jnp.where(first_half, pltpu.roll(y, LANES - HEAD_DIM // 2, 1), pltpu.roll(y, HEAD_DIM // 2, 1))
        return y * cos + partner * sin_signed

    def head_norm(y, g):
        ms = _split_dot(y * y, seg) * (1.0 / HEAD_DIM)
        return (y * lax.rsqrt(ms + EPS)) * g

    def slabs(acc):
        return [acc[:, s * LANES:(s + 1) * LANES] for s in range(acc.shape[1] // LANES)]

    scale = HEAD_DIM ** -0.5
    gq, gk = gq_ref[...], gk_ref[...]
    n_qt = qa_ref.shape[0]
    tq = qa_ref.shape[2]
    for s, y in enumerate(slabs(proj(0, 512))):
        yt = (rope(head_norm(y, gq), cosa, sina) * (scale * LOG2_E)).T.astype(BF16)
        for t in range(n_qt):
            qa_ref[t, s * LANES:(s + 1) * LANES, :] = yt[:, t * tq:(t + 1) * tq]
    ka_ref[...] = rope(head_norm(proj(512, 128), gk), cosa, sina).astype(BF16)
    va_ref[0] = proj(640, 128).T.astype(BF16)
    for s, y in enumerate(slabs(proj(768, 256))):
        qb_ref[:, s * LANES:(s + 1) * LANES] = (rope(y, coss, sins) * scale).astype(BF16)
    kb_ref[...] = rope(proj(1024, 128), coss, sins).astype(BF16)
    vb_ref[...] = proj(1152, 128).astype(BF16)
    for s, y in enumerate(slabs(proj(1280, 256))):
        qc_ref[:, s * LANES:(s + 1) * LANES] = rope(y, coss, sins).astype(BF16)
    for s, y in enumerate(slabs(proj(1536, 256))):
        kc_ref[:, s * LANES:(s + 1) * LANES] = (rope(y, coss, sins) * scale).astype(BF16)
    vc_ref[...] = proj(1792, 256).astype(BF16)
    gate = proj(2048, 256)
    gc_ref[...] = (gate / (1.0 + jnp.exp(-gate))).astype(BF16)


def _in_proj(x, g, w, gq, gk, tabs, seg, seq_len, tm=A_TKS):
    n = x.shape[0]
    nt = seq_len // tm
    row = lambda i: (i, 0)
    const = lambda i: (0, 0)
    tab = lambda i: (i % nt, 0)
    widths = (128, 256, 128, 128, 256, 256, 256, 256)
    lead = lambda i: (i, 0, 0)
    out_specs = ([pl.BlockSpec((tm // A_TQ, A_Q, A_TQ), lead), pl.BlockSpec((tm, LANES), row),
                  pl.BlockSpec((1, LANES, tm), lead)] + [pl.BlockSpec((tm, wd), row) for wd in widths[1:]])
    out_shape = ([jax.ShapeDtypeStruct((n // A_TQ, A_Q, A_TQ), BF16), jax.ShapeDtypeStruct((n, LANES), BF16),
                  jax.ShapeDtypeStruct((n // tm, LANES, tm), BF16)]
                 + [jax.ShapeDtypeStruct((n, wd), BF16) for wd in widths[1:]])
    return pl.pallas_call(
        _in_proj_kernel,
        grid=(n // tm,),
        in_specs=[pl.BlockSpec((tm, D_MODEL), row), pl.BlockSpec((1, D_MODEL), const),
                  pl.BlockSpec((D_MODEL, IN_W), const), pl.BlockSpec((1, LANES), const),
                  pl.BlockSpec((1, LANES), const)]
                 + [pl.BlockSpec((tm, LANES), tab)] * 4 + [pl.BlockSpec((LANES, LANES), const)],
        out_specs=out_specs,
        out_shape=out_shape,
        compiler_params=pltpu.CompilerParams(dimension_semantics=("parallel",), vmem_limit_bytes=VMEM_LIMIT),
        name="in_proj",
    )(x, g, w, gq, gk, *tabs, seg)


def _attn_a_kernel(q_ref, k_ref, v_ref, g_ref, o_ref, qpad_ref, m_ref, l_ref, acc_ref):
    tq = q_ref.shape[1]
    n_sub, _, tks = v_ref.shape
    tki = min(A_TKI, n_sub * tks)
    hd = HEAD_DIM
    heads_per_kv = A_HEADS // 2

    zeros = jnp.zeros((hd, tq), BF16)
    for h in range(A_HEADS):
        qh = q_ref[h * hd:(h + 1) * hd, :]
        qpad_ref[h] = jnp.concatenate([qh, zeros] if h < heads_per_kv else [zeros, qh], axis=0)
    m_ref[...] = jnp.full(m_ref.shape, MASK_VALUE, F32)
    l_ref[...] = jnp.zeros(l_ref.shape, F32)
    acc_ref[...] = jnp.zeros(acc_ref.shape, F32)

    vper = tki // tks

    def key_step(c, carry):
        kt = k_ref[pl.ds(pl.multiple_of(c * tki, tki), tki), :]
        vts = [v_ref[c * vper + j] for j in range(vper)]

        def scores(h):
            return jnp.dot(kt, qpad_ref[h], preferred_element_type=F32)

        def softmax(h, s):
            m_prev = m_ref[h:h + 1, :]
            m_new = jnp.maximum(m_prev, jnp.max(s, axis=0, keepdims=True))
            alpha = jnp.exp2(m_prev - m_new)
            p = jnp.exp2(s - m_new)
            l_ref[h:h + 1, :] = alpha * l_ref[h:h + 1, :] + jnp.sum(p, axis=0, keepdims=True)
            m_ref[h:h + 1, :] = m_new
            return p.astype(BF16), alpha

        def accumulate(h, p, alpha):
            kv = h // heads_per_kv
            rows = slice(h * hd, (h + 1) * hd)
            pv = alpha * acc_ref[rows, :]
            for j, vt in enumerate(vts):
                pv = pv + jnp.dot(vt[kv * hd:(kv + 1) * hd, :], p[j * tks:(j + 1) * tks, :],
                                  preferred_element_type=F32)
            acc_ref[rows, :] = pv

        s = scores(0)
        pending = None
        for h in range(A_HEADS):
            s_next = scores(h + 1) if h + 1 < A_HEADS else None
            p, alpha = softmax(h, s)
            if pending is not None:
                accumulate(*pending)
            pending = (h, p, alpha)
            s = s_next
        accumulate(*pending)
        return carry

    lax.fori_loop(0, n_sub // vper, key_step, 0)

    out_t = jnp.concatenate(
        [acc_ref[h * hd:(h + 1) * hd, :] / l_ref[h:h + 1, :] for h in range(A_HEADS)], axis=0)
    ms = jnp.mean(out_t * out_t, axis=0, keepdims=True)
    out_t = (out_t * lax.rsqrt(ms + EPS)) * g_ref[...]
    o_ref[...] = out_t.T.astype(BF16)


def _attn_a(q_t, k, v_t, g_col, b, t):
    tq, tks = A_TQ, A_TKS
    nq, ns = t // tq, t // tks
    return pl.pallas_call(
        _attn_a_kernel,
        grid=(b, nq),
        in_specs=[pl.BlockSpec((None, A_Q, tq), lambda bi, i: (bi * nq + i, 0, 0)),
                  pl.BlockSpec((None, t, LANES), lambda bi, i: (bi, 0, 0)),
                  pl.BlockSpec((ns, LANES, tks), lambda bi, i: (bi, 0, 0)),
                  pl.BlockSpec((A_Q, 1), lambda bi, i: (0, 0))],
        out_specs=pl.BlockSpec((None, tq, A_Q), lambda bi, i: (bi, i, 0)),
        out_shape=jax.ShapeDtypeStruct((b, t, A_Q), BF16),
        scratch_shapes=[pltpu.VMEM((A_HEADS, 2 * HEAD_DIM, tq), BF16), pltpu.VMEM((A_HEADS, tq), F32),
                        pltpu.VMEM((A_HEADS, tq), F32), pltpu.VMEM((A_Q, tq), F32)],
        compiler_params=pltpu.CompilerParams(dimension_semantics=("parallel", "parallel"),
                                             vmem_limit_bytes=VMEM_LIMIT),
        name="attn_a",
    )(q_t, k, v_t, g_col)


def _attn_b_kernel(sink_ref, q_ref, kp_ref, km_ref, kn_ref, vp_ref, vm_ref, vn_ref, g_ref, o_ref,
                   kw_ref, vw_ref, *, seq_len):
    tq = q_ref.shape[0]
    w = WINDOW
    i = pl.program_id(1)
    kw_ref[0:w, :] = kp_ref[...]
    kw_ref[w:w + tq, :] = km_ref[...]
    kw_ref[w + tq:2 * w + tq, :] = kn_ref[...]
    vw_ref[0:w, :] = vp_ref[...]
    vw_ref[w:w + tq, :] = vm_ref[...]
    vw_ref[w + tq:2 * w + tq, :] = vn_ref[...]
    qi = lax.broadcasted_iota(jnp.int32, (w, 3 * w), 0)
    kj = lax.broadcasted_iota(jnp.int32, (w, 3 * w), 1)
    band = jnp.abs(kj - w - qi) <= w
    left = _left_lanes(w)
    for c in range(tq // w):
        kpos = (i * (tq // w) + c - 1) * w + kj
        valid = band & (kpos >= 0) & (kpos < seq_len)
        kwin = kw_ref[c * w:(c + 3) * w, :]
        vwin = vw_ref[c * w:(c + 3) * w, :]
        qf = q_ref[c * w:(c + 1) * w, :].astype(F32)
        s0, s1 = qf[:, :LANES], qf[:, LANES:]
        qh = (jnp.where(left, s0, 0.0), pltpu.roll(jnp.where(left, 0.0, s0), HEAD_DIM, 1),
              pltpu.roll(jnp.where(left, s1, 0.0), HEAD_DIM, 1), jnp.where(left, 0.0, s1))
        outs = []
        for hd in range(B_HEADS):
            s = lax.dot_general(qh[hd].astype(BF16), kwin, NT_DIMS, preferred_element_type=F32)
            s = jnp.where(valid, s, MASK_VALUE)
            sink = sink_ref[hd]
            m = jnp.maximum(jnp.max(s, axis=1, keepdims=True), sink)
            p = jnp.exp(s - m)
            denom = jnp.sum(p, axis=1, keepdims=True) + jnp.exp(sink - m)
            outs.append(jnp.dot(p.astype(BF16), vwin, preferred_element_type=F32) / denom)
        slab0 = jnp.where(left, outs[0], pltpu.roll(outs[1], HEAD_DIM, 1))
        slab1 = jnp.where(left, pltpu.roll(outs[2], HEAD_DIM, 1), outs[3])
        o = _rms(jnp.concatenate([slab0, slab1], axis=1), g_ref[...])
        o_ref[c * w:(c + 1) * w, :] = o.astype(BF16)


def _attn_b(q, k, v, sink, g, tq=512):
    b, t, _ = q.shape
    w = WINDOW
    r = tq // w
    nb = t // w
    main = lambda bi, i: (bi, i, 0)
    prev = lambda bi, i: (bi, jnp.maximum(i * r - 1, 0), 0)
    nxt = lambda bi, i: (bi, jnp.minimum(i * r + r, nb - 1), 0)
    kv_specs = [pl.BlockSpec((None, w, LANES), prev), pl.BlockSpec((None, tq, LANES), main),
                pl.BlockSpec((None, w, LANES), nxt)]
    return pl.pallas_call(
        functools.partial(_attn_b_kernel, seq_len=t),
        grid=(b, t // tq),
        in_specs=[pl.BlockSpec(memory_space=pltpu.SMEM), pl.BlockSpec((None, tq, B_Q), main)]
                 + kv_specs + kv_specs + [pl.BlockSpec((1, B_Q), lambda bi, i: (0, 0))],
        out_specs=pl.BlockSpec((None, tq, B_Q), main),
        out_shape=jax.ShapeDtypeStruct((b, t, B_Q), BF16),
        scratch_shapes=[pltpu.VMEM((tq + 2 * w, LANES), BF16), pltpu.VMEM((tq + 2 * w, LANES), BF16)],
        compiler_params=pltpu.CompilerParams(dimension_semantics=("parallel", "parallel"),
                                             vmem_limit_bytes=VMEM_LIMIT),
        name="attn_b",
    )(sink, q, k, k, k, v, v, v, g)


def _ret_fwd_kernel(q_ref, k_ref, v_ref, mask_ref, dq_ref, dk_ref, dec_ref, bm_ref, o_ref, state_ref, *, cps):
    c_len = CHUNK

    @pl.when(pl.program_id(1) == 0)
    def _():
        state_ref[...] = jnp.zeros(state_ref.shape, F32)

    lane_head = lax.broadcasted_iota(jnp.int32, (c_len, C_W), 1) // HEAD_DIM
    for c in range(cps):
        rows = slice(c * c_len, (c + 1) * c_len)
        q = q_ref[rows, :].astype(F32)
        k = k_ref[rows, :]
        v = v_ref[rows, :]
        o = jnp.dot((q * dq_ref[...]).astype(BF16), state_ref[...].astype(BF16), preferred_element_type=F32)
        for hd in range(C_HEADS):
            sel = lane_head == hd
            qm = jnp.where(sel, q, 0.0).astype(BF16)
            s = lax.dot_general(qm, k, NT_DIMS, preferred_element_type=F32) * mask_ref[hd]
            o = o + jnp.where(sel, jnp.dot(s.astype(BF16), v, preferred_element_type=F32), 0.0)
        o_ref[rows, :] = o
        kd = (k.astype(F32) * dk_ref[...]).astype(BF16)
        kv = lax.dot_general(kd, v, TN_DIMS, preferred_element_type=F32)
        state_ref[...] = state_ref[...] * dec_ref[...] + kv * bm_ref[...]


def _ret_bwd_kernel(q_ref, k_ref, v_ref, part_ref, gate_ref, dq_ref, dk_ref, dec_ref, bm_ref, seg_ref, g_ref,
                    o_ref, state_ref, *, cps):
    c_len = CHUNK

    @pl.when(pl.program_id(1) == 0)
    def _():
        state_ref[...] = jnp.zeros(state_ref.shape, F32)

    seg = seg_ref[...]
    for c in reversed(range(cps)):
        rows = slice(c * c_len, (c + 1) * c_len)
        q = q_ref[rows, :].astype(F32)
        k = k_ref[rows, :]
        v = v_ref[rows, :]
        o = part_ref[rows, :] + jnp.dot((q * dq_ref[...]).astype(BF16), state_ref[...].astype(BF16),
                                        preferred_element_type=F32)
        kd = (k.astype(F32) * dk_ref[...]).astype(BF16)
        kv = lax.dot_general(kd, v, TN_DIMS, preferred_element_type=F32)
        state_ref[...] = state_ref[...] * dec_ref[...] + kv * bm_ref[...]
        oc = o - _split_dot(o, seg) * (1.0 / HEAD_DIM)
        var = _split_dot(oc * oc, seg) * (1.0 / HEAD_DIM)
        on = (oc * lax.rsqrt(var + EPS)) * g_ref[...]
        o_ref[rows, :] = (on * gate_ref[rows, :].astype(F32)).astype(BF16)


def _retention(q, k, v, gate, consts, g_gn, cps=4):
    b, t, _ = q.shape
    mask, dqf, dkf, decf, dqb, dkb, decb, bmask, seg = consts
    tr = cps * CHUNK
    ns = t // tr
    fwd = lambda bi, i: (bi, i, 0)
    bwd = lambda bi, i: (bi, ns - 1 - i, 0)
    c2 = lambda bi, i: (0, 0)
    c3 = lambda bi, i: (0, 0, 0)
    params = pltpu.CompilerParams(dimension_semantics=("parallel", "arbitrary"), vmem_limit_bytes=VMEM_LIMIT)
    part = pl.pallas_call(
        functools.partial(_ret_fwd_kernel, cps=cps),
        grid=(b, ns),
        in_specs=[pl.BlockSpec((None, tr, C_W), fwd)] * 3
                 + [pl.BlockSpec((C_HEADS, CHUNK, CHUNK), c3), pl.BlockSpec((CHUNK, C_W), c2),
                    pl.BlockSpec((CHUNK, C_W), c2), pl.BlockSpec((1, C_W), c2), pl.BlockSpec((C_W, C_W), c2)],
        out_specs=pl.BlockSpec((None, tr, C_W), fwd),
        out_shape=jax.ShapeDtypeStruct((b, t, C_W), F32),
        scratch_shapes=[pltpu.VMEM((C_W, C_W), F32)],
        compiler_params=params,
        name="retention_fwd",
    )(q, k, v, mask, dqf, dkf, decf, bmask)
    return pl.pallas_call(
        functools.partial(_ret_bwd_kernel, cps=cps),
        grid=(b, ns),
        in_specs=[pl.BlockSpec((None, tr, C_W), bwd)] * 5
                 + [pl.BlockSpec((CHUNK, C_W), c2), pl.BlockSpec((CHUNK, C_W), c2), pl.BlockSpec((1, C_W), c2),
                    pl.BlockSpec((C_W, C_W), c2), pl.BlockSpec((C_W, C_W), c2), pl.BlockSpec((1, C_W), c2)],
        out_specs=pl.BlockSpec((None, tr, C_W), bwd),
        out_shape=jax.ShapeDtypeStruct((b, t, C_W), BF16),
        scratch_shapes=[pltpu.VMEM((C_W, C_W), F32)],
        compiler_params=params,
        name="retention_bwd",
    )(q, k, v, part, gate, dqb, dkb, decb, bmask, seg, g_gn)


def _mem_proj_kernel(m_ref, g_ref, wk_ref, wv_ref, k_ref, v_ref):
    h = _rms(m_ref[...], g_ref[...]).astype(BF16)
    k_ref[...] = jnp.dot(h, wk_ref[...], preferred_element_type=F32).astype(BF16)
    v_ref[...] = jnp.dot(h, wv_ref[...], preferred_element_type=F32).astype(BF16)


def _mem_proj(mem, g, wk, wv, tm=256):
    n = mem.shape[0]
    row = lambda i: (i, 0)
    const = lambda i: (0, 0)
    return pl.pallas_call(
        _mem_proj_kernel,
        grid=(n // tm,),
        in_specs=[pl.BlockSpec((tm, D_MODEL), row), pl.BlockSpec((1, D_MODEL), const),
                  pl.BlockSpec((D_MODEL, D_MODEL), const), pl.BlockSpec((D_MODEL, D_MODEL), const)],
        out_specs=[pl.BlockSpec((tm, D_MODEL), row)] * 2,
        out_shape=[jax.ShapeDtypeStruct((n, D_MODEL), BF16)] * 2,
        compiler_params=pltpu.CompilerParams(dimension_semantics=("parallel",), vmem_limit_bytes=VMEM_LIMIT),
        name="mem_proj",
    )(mem, g, wk, wv)


def _mid_kernel(x_ref, oa_ref, ob_ref, oc_ref, wo_ref, g_ref, wq_ref, km_ref, vm_ref, wxo_ref, o_ref):
    x1 = (x_ref[...]
          + jnp.dot(oa_ref[...], wo_ref[0:A_Q, :], preferred_element_type=F32)
          + jnp.dot(ob_ref[...], wo_ref[A_Q:A_Q + B_Q, :], preferred_element_type=F32)
          + jnp.dot(oc_ref[...], wo_ref[A_Q + B_Q:, :], preferred_element_type=F32))
    h = _rms(x1, g_ref[...]).astype(BF16)
    q = (jnp.dot(h, wq_ref[...], preferred_element_type=F32) * (X_HEAD_DIM ** -0.5)).astype(BF16)
    heads = []
    for hd in range(X_HEADS):
        cols = slice(hd * X_HEAD_DIM, (hd + 1) * X_HEAD_DIM)
        s = lax.dot_general(q[:, cols], km_ref[:, cols], NT_DIMS, preferred_element_type=F32)
        p = jnp.exp(s - jnp.max(s, axis=1, keepdims=True))
        o = jnp.dot(p.astype(BF16), vm_ref[:, cols], preferred_element_type=F32)
        heads.append((o / jnp.sum(p, axis=1, keepdims=True)).astype(BF16))
    o_ref[...] = x1 + jnp.dot(jnp.concatenate(heads, axis=1), wxo_ref[...], preferred_element_type=F32)


def _mid(x, oa, ob, oc, wo, g, wq, km, vm, wxo, tm=512):
    b, t, _ = x.shape
    mlen = km.shape[1]
    row = lambda bi, i: (bi, i, 0)
    const = lambda bi, i: (0, 0)
    per_b = lambda bi, i: (bi, 0, 0)
    wspec = pl.BlockSpec((D_MODEL, D_MODEL), const)
    return pl.pallas_call(
        _mid_kernel,
        grid=(b, t // tm),
        in_specs=[pl.BlockSpec((None, tm, D_MODEL), row), pl.BlockSpec((None, tm, A_Q), row),
                  pl.BlockSpec((None, tm, B_Q), row), pl.BlockSpec((None, tm, C_W), row),
                  wspec, pl.BlockSpec((1, D_MODEL), const), wspec,
                  pl.BlockSpec((None, mlen, D_MODEL), per_b), pl.BlockSpec((None, mlen, D_MODEL), per_b), wspec],
        out_specs=pl.BlockSpec((None, tm, D_MODEL), row),
        out_shape=jax.ShapeDtypeStruct((b, t, D_MODEL), F32),
        compiler_params=pltpu.CompilerParams(dimension_semantics=("parallel", "parallel"),
                                             vmem_limit_bytes=VMEM_LIMIT),
        name="mid",
    )(x, oa, ob, oc, wo, g, wq, km, vm, wxo)


def _ffn_kernel(x_ref, g_ref, wg_ref, wu_ref, wd_ref, gf_ref, o_ref, h_ref, *, final_norm):
    j = pl.program_id(1)

    @pl.when(j == 0)
    def _():
        x = x_ref[...]
        h_ref[...] = _rms(x, g_ref[...]).astype(BF16)
        o_ref[...] = x

    h = h_ref[...]
    a = jnp.dot(h, wg_ref[...], preferred_element_type=F32)
    u = jnp.dot(h, wu_ref[...], preferred_element_type=F32)
    t = ((a / (1.0 + jnp.exp(-a))) * u).astype(BF16)
    o_ref[...] += jnp.dot(t, wd_ref[...], preferred_element_type=F32)

    if final_norm:
        @pl.when(j == pl.num_programs(1) - 1)
        def _():
            o_ref[...] = _rms(o_ref[...], gf_ref[...])


def _ffn(x, g, wg, wu, wd, gf, final_norm, tm=512, tf=1408):
    n = x.shape[0]
    row = lambda i, j: (i, 0)
    const = lambda i, j: (0, 0)
    return pl.pallas_call(
        functools.partial(_ffn_kernel, final_norm=final_norm),
        grid=(n // tm, D_FF // tf),
        in_specs=[pl.BlockSpec((tm, D_MODEL), row), pl.BlockSpec((1, D_MODEL), const),
                  pl.BlockSpec((D_MODEL, tf), lambda i, j: (0, j)), pl.BlockSpec((D_MODEL, tf), lambda i, j: (0, j)),
                  pl.BlockSpec((tf, D_MODEL), lambda i, j: (j, 0)), pl.BlockSpec((1, D_MODEL), const)],
        out_specs=pl.BlockSpec((tm, D_MODEL), row),
        out_shape=jax.ShapeDtypeStruct((n, D_MODEL), F32),
        scratch_shapes=[pltpu.VMEM((tm, D_MODEL), BF16)],
        compiler_params=pltpu.CompilerParams(dimension_semantics=("parallel", "arbitrary"),
                                             vmem_limit_bytes=VMEM_LIMIT),
        name="ffn",
    )(x, g, wg, wu, wd, gf)


def _rope_freqs(pos, dim):
    inv = ROPE_THETA ** (-jnp.arange(0, dim, 2, dtype=F32) / dim)
    return pos.astype(F32)[:, None] * inv[None, :]


def _rope_tables(t):
    rows = t // GRID_W
    row = jnp.repeat(jnp.arange(rows), GRID_W)
    col = jnp.tile(jnp.arange(GRID_W), rows)
    ang_axial = jnp.concatenate([_rope_freqs(row, HEAD_DIM // 2), _rope_freqs(col, HEAD_DIM // 2)], axis=-1)
    ang_seq = _rope_freqs(jnp.arange(t), HEAD_DIM)

    def lanes(ang):
        cos, sin = jnp.cos(ang), jnp.sin(ang)
        reps = LANES // HEAD_DIM
        return (jnp.tile(jnp.concatenate([cos, cos], axis=-1), (1, reps)),
                jnp.tile(jnp.concatenate([-sin, sin], axis=-1), (1, reps)))

    return lanes(ang_axial) + lanes(ang_seq)


def _retention_consts(p_f, p_b, bmask, seg):
    c = CHUNK
    lgf = -jnp.exp(p_f.astype(F32))
    lgb = -jnp.exp(p_b.astype(F32))
    idx = jnp.arange(c, dtype=F32)
    diff = idx[:, None] - idx[None, :]
    mask = jnp.where(diff >= 0,
                     jnp.exp(lgf[:, None, None] * jnp.maximum(diff, 0.0)),
                     jnp.exp(lgb[:, None, None] * jnp.maximum(-diff, 0.0)))
    lf = jnp.repeat(lgf, HEAD_DIM)[None, :]
    lb = jnp.repeat(lgb, HEAD_DIM)[None, :]
    dqf = jnp.exp(lf * (idx + 1.0)[:, None])
    dkf = jnp.exp(lf * (c - 1 - idx)[:, None])
    decf = jnp.exp(lf * c)
    dqb = jnp.exp(lb * (c - idx)[:, None])
    dkb = jnp.exp(lb * idx[:, None])
    decb = jnp.exp(lb * c)
    return mask, dqf, dkf, decf, dqb, dkb, decb, bmask, seg


def _block_ones(n):
    r = jnp.arange(n) // HEAD_DIM
    return r[:, None] == r[None, :]


def _trunk(x, mem, weights, layer_consts, seg128, g_final):
    b, t, _ = x.shape
    n = b * t
    tabs = _rope_tables(t)
    mem2 = mem.reshape(b * mem.shape[1], D_MODEL)
    xf = x.reshape(n, D_MODEL)
    for l in range(DEPTH):
        wl = weights[l]
        qa, ka, va, qb, kb, vb, qc, kc, vc, gc = _in_proj(
            xf, wl["g_mix"], wl["w_in"], wl["gq"], wl["gk"], tabs, seg128, t)
        r3 = lambda a: a.reshape(b, t, a.shape[-1])
        oa = _attn_a(qa, r3(ka), va, wl["a_out_norm"], b, t)
        ob = _attn_b(r3(qb), r3(kb), r3(vb), wl["b_sink"], wl["b_out_norm"])
        oc = _retention(r3(qc), r3(kc), r3(vc), r3(gc), layer_consts[l], wl["c_gn"])
        km, vm = _mem_proj(mem2, wl["g_mem"], wl["w_xk"], wl["w_xv"])
        km = km.reshape(b, -1, D_MODEL)
        vm = vm.reshape(b, -1, D_MODEL)
        x2 = _mid(xf.reshape(b, t, D_MODEL), oa, ob, oc, wl["w_out"], wl["g_cross"], wl["w_xq"], km, vm,
                  wl["w_xo"])
        xf = _ffn(x2.reshape(n, D_MODEL), wl["g_ffn"], wl["w_gate"], wl["w_up"], wl["w_down"], g_final,
                  final_norm=(l == DEPTH - 1))
    return xf.reshape(b, t, D_MODEL)


def kernel(x_prompt, x_sample, mem_prompt, mem_sample, g_mix, w_in, a_q_norm, a_k_norm, a_out_norm, b_sink,
           b_out_norm, c_decay_fwd, c_decay_bwd, c_gn, w_out, g_cross, g_mem, w_xq, w_xk, w_xv, w_xo, g_ffn,
           w_gate, w_up, w_down, g_final):
    row = lambda a: a.astype(F32).reshape(1, -1)
    reps = LANES // HEAD_DIM
    seg128 = _block_ones(LANES).astype(BF16)
    seg256 = _block_ones(C_W).astype(BF16)
    bmask = _block_ones(C_W).astype(F32)
    weights, layer_consts = [], []
    for l in range(DEPTH):
        weights.append(dict(
            g_mix=row(g_mix[l]), w_in=w_in[l].astype(BF16),
            gq=jnp.tile(row(a_q_norm[l]), (1, reps)), gk=jnp.tile(row(a_k_norm[l]), (1, reps)),
            a_out_norm=a_out_norm[l].astype(F32).reshape(-1, 1), b_sink=b_sink[l].astype(F32), b_out_norm=row(b_out_norm[l]),
            c_gn=row(c_gn[l]), w_out=w_out[l].astype(BF16), g_cross=row(g_cross[l]), g_mem=row(g_mem[l]),
            w_xq=w_xq[l].astype(BF16), w_xk=w_xk[l].astype(BF16), w_xv=w_xv[l].astype(BF16),
            w_xo=w_xo[l].astype(BF16), g_ffn=row(g_ffn[l]), w_gate=w_gate[l].astype(BF16),
            w_up=w_up[l].astype(BF16), w_down=w_down[l].astype(BF16)))
        layer_consts.append(_retention_consts(c_decay_fwd[l], c_decay_bwd[l], bmask, seg256))
    gf = row(g_final)
    y_prompt = _trunk(x_prompt, mem_prompt, weights, layer_consts, seg128, gf)
    y_sample = _trunk(x_sample, mem_sample, weights, layer_consts, seg128, gf)
    return (y_prompt, y_sample)
```

```python
import functools

import jax
import jax.numpy as jnp
from jax import lax
from jax.experimental import pallas as pl
from jax.experimental.pallas import tpu as pltpu

F32 = jnp.float32
BF16 = jnp.bfloat16

D_MODEL = 1024
DEPTH = 4
HEAD_DIM = 64
GRID_W = 64
A_HEADS = 8
B_HEADS = 4
C_HEADS = 4
WINDOW = 128
CHUNK = 128
ROPE_THETA = 10000.0
X_HEADS = 4
X_HEAD_DIM = D_MODEL // X_HEADS
D_FF = 2816
EPS = 1e-6
A_Q = A_HEADS * HEAD_DIM
B_Q = B_HEADS * HEAD_DIM
C_W = C_HEADS * HEAD_DIM
IN_W = 2304
MASK_VALUE = -1e30

LOG2_E = 1.4426950408889634

LANES = 128
A_TQ = 256
A_TKS = 512
A_TKI = 1024
VMEM_LIMIT = 56 * 1024 * 1024

NT_DIMS = (((1,), (1,)), ((), ()))
TN_DIMS = (((0,), (0,)), ((), ()))


def _rms(x, g):
    return (x * lax.rsqrt(jnp.mean(x * x, axis=-1, keepdims=True) + EPS)) * g


def _split_dot(x, m):
    hi = x.astype(BF16)
    lo = (x - hi.astype(F32)).astype(BF16)
    return jnp.dot(hi, m, preferred_element_type=F32) + jnp.dot(lo, m, preferred_element_type=F32)


def _left_lanes(rows):
    return lax.broadcasted_iota(jnp.int32, (rows, LANES), 1) < HEAD_DIM


def _in_proj_kernel(x_ref, g_ref, w_ref, gq_ref, gk_ref, cosa_ref, sina_ref, coss_ref, sins_ref, seg_ref,
                    qa_ref, ka_ref, va_ref, qb_ref, kb_ref, vb_ref, qc_ref, kc_ref, vc_ref, gc_ref):
    tm = x_ref.shape[0]
    h = _rms(x_ref[...], g_ref[...]).astype(BF16)
    lane = lax.broadcasted_iota(jnp.int32, (tm, LANES), 1)
    first_half = (lane & (HEAD_DIM // 2)) == 0
    seg = seg_ref[...]
    cosa, sina = cosa_ref[...], sina_ref[...]
    coss, sins = coss_ref[...], sins_ref[...]

    def proj(c0, width):
        return jnp.dot(h, w_ref[:, c0:c0 + width], preferred_element_type=F32)

    def rope(y, cos, sin_signed):
        partner = jnp.where(first_half, pltpu.roll(y, LANES - HEAD_DIM // 2, 1), pltpu.roll(y, HEAD_DIM // 2, 1))
        return y * cos + partner * sin_signed

    def head_norm(y, g):
        ms = _split_dot(y * y, seg) * (1.0 / HEAD_DIM)
        return (y * lax.rsqrt(ms + EPS)) * g

    def slabs(acc):
        return [acc[:, s * LANES:(s + 1) * LANES] for s in range(acc.shape[1] // LANES)]

    scale = HEAD_DIM ** -0.5
    gq, gk = gq_ref[...], gk_ref[...]
    n_qt = qa_ref.shape[0]
    tq = qa_ref.shape[2]
    for s, y in enumerate(slabs(proj(0, 512))):
        yt = (rope(head_norm(y, gq), cosa, sina) * (scale * LOG2_E)).T.astype(BF16)
        for t in range(n_qt):
            qa_ref[t, s * LANES:(s + 1) * LANES, :] = yt[:, t * tq:(t + 1) * tq]
    ka_ref[...] = rope(head_norm(proj(512, 128), gk), cosa, sina).astype(BF16)
    va_ref[0] = proj(640, 128).T.astype(BF16)
    for s, y in enumerate(slabs(proj(768, 256))):
        qb_ref[:, s * LANES:(s + 1) * LANES] = (rope(y, coss, sins) * scale).astype(BF16)
    kb_ref[...] = rope(proj(1024, 128), coss, sins).astype(BF16)
    vb_ref[...] = proj(1152, 128).astype(BF16)
    for s, y in enumerate(slabs(proj(1280, 256))):
        qc_ref[:, s * LANES:(s + 1) * LANES] = rope(y, coss, sins).astype(BF16)
    for s, y in enumerate(slabs(proj(1536, 256))):
        kc_ref[:, s * LANES:(s + 1) * LANES] = (rope(y, coss, sins) * scale).astype(BF16)
    vc_ref[...] = proj(1792, 256).astype(BF16)
    gate = proj(2048, 256)
    gc_ref[...] = (gate / (1.0 + jnp.exp(-gate))).astype(BF16)


def _in_proj(x, g, w, gq, gk, tabs, seg, seq_len, tm=A_TKS):
    n = x.shape[0]
    nt = seq_len // tm
    row = lambda i: (i, 0)
    const = lambda i: (0, 0)
    tab = lambda i: (i % nt, 0)
    widths = (128, 256, 128, 128, 256, 256, 256, 256)
    lead = lambda i: (i, 0, 0)
    out_specs = ([pl.BlockSpec((tm // A_TQ, A_Q, A_TQ), lead), pl.BlockSpec((tm, LANES), row),
                  pl.BlockSpec((1, LANES, tm), lead)] + [pl.BlockSpec((tm, wd), row) for wd in widths[1:]])
    out_shape = ([jax.ShapeDtypeStruct((n // A_TQ, A_Q, A_TQ), BF16), jax.ShapeDtypeStruct((n, LANES), BF16),
                  jax.ShapeDtypeStruct((n // tm, LANES, tm), BF16)]
                 + [jax.ShapeDtypeStruct((n, wd), BF16) for wd in widths[1:]])
    return pl.pallas_call(
        _in_proj_kernel,
        grid=(n // tm,),
        in_specs=[pl.BlockSpec((tm, D_MODEL), row), pl.BlockSpec((1, D_MODEL), const),
                  pl.BlockSpec((D_MODEL, IN_W), const), pl.BlockSpec((1, LANES), const),
                  pl.BlockSpec((1, LANES), const)]
                 + [pl.BlockSpec((tm, LANES), tab)] * 4 + [pl.BlockSpec((LANES, LANES), const)],
        out_specs=out_specs,
        out_shape=out_shape,
        compiler_params=pltpu.CompilerParams(dimension_semantics=("parallel",), vmem_limit_bytes=VMEM_LIMIT),
        name="in_proj",
    )(x, g, w, gq, gk, *tabs, seg)


def _attn_a_kernel(q_ref, k_ref, v_ref, g_ref, o_ref, qpad_ref, m_ref, l_ref, acc_ref):
    tq = q_ref.shape[1]
    n_sub, _, tks = v_ref.shape
    tki = min(A_TKI, n_sub * tks)
    hd = HEAD_DIM
    heads_per_kv = A_HEADS // 2

    zeros = jnp.zeros((hd, tq), BF16)
    for h in range(A_HEADS):
        qh = q_ref[h * hd:(h + 1) * hd, :]
        qpad_ref[h] = jnp.concatenate([qh, zeros] if h < heads_per_kv else [zeros, qh], axis=0)
    m_ref[...] = jnp.full(m_ref.shape, MASK_VALUE, F32)
    l_ref[...] = jnp.zeros(l_ref.shape, F32)
    acc_ref[...] = jnp.zeros(acc_ref.shape, F32)

    vper = tki // tks

    def key_step(c, carry):
        kt = k_ref[pl.ds(pl.multiple_of(c * tki, tki), tki), :]
        vts = [v_ref[c * vper + j] for j in range(vper)]

        def scores(h):
            return jnp.dot(kt, qpad_ref[h], preferred_element_type=F32)

        def softmax(h, s):
            m_prev = m_ref[h:h + 1, :]
            m_new = jnp.maximum(m_prev, jnp.max(s, axis=0, keepdims=True))
            alpha = jnp.exp2(m_prev - m_new)
            p = jnp.exp2(s - m_new)
            l_ref[h:h + 1, :] = alpha * l_ref[h:h + 1, :] + jnp.sum(p, axis=0, keepdims=True)
            m_ref[h:h + 1, :] = m_new
            return p.astype(BF16), alpha

        def accumulate(h, p, alpha):
            kv = h // heads_per_kv
            rows = slice(h * hd, (h + 1) * hd)
            pv = alpha * acc_ref[rows, :]
            for j, vt in enumerate(vts):
                pv = pv + jnp.dot(vt[kv * hd:(kv + 1) * hd, :], p[j * tks:(j + 1) * tks, :],
                                  preferred_element_type=F32)
            acc_ref[rows, :] = pv

        s = scores(0)
        pending = None
        for h in range(A_HEADS):
            s_next = scores(h + 1) if h + 1 < A_HEADS else None
            p, alpha = softmax(h, s)
            if pending is not None:
                accumulate(*pending)
            pending = (h, p, alpha)
            s = s_next
        accumulate(*pending)
        return carry

    lax.fori_loop(0, n_sub // vper, key_step, 0, unroll=2)

    out_t = jnp.concatenate(
        [acc_ref[h * hd:(h + 1) * hd, :] / l_ref[h:h + 1, :] for h in range(A_HEADS)], axis=0)
    ms = jnp.mean(out_t * out_t, axis=0, keepdims=True)
    out_t = (out_t * lax.rsqrt(ms + EPS)) * g_ref[...]
    o_ref[...] = out_t.T.astype(BF16)


def _attn_a(q_t, k, v_t, g_col, b, t):
    tq, tks = A_TQ, A_TKS
    nq, ns = t // tq, t // tks
    return pl.pallas_call(
        _attn_a_kernel,
        grid=(b, nq),
        in_specs=[pl.BlockSpec((None, A_Q, tq), lambda bi, i: (bi * nq + i, 0, 0)),
                  pl.BlockSpec((None, t, LANES), lambda bi, i: (bi, 0, 0)),
                  pl.BlockSpec((ns, LANES, tks), lambda bi, i: (bi, 0, 0)),
                  pl.BlockSpec((A_Q, 1), lambda bi, i: (0, 0))],
        out_specs=pl.BlockSpec((None, tq, A_Q), lambda bi, i: (bi, i, 0)),
        out_shape=jax.ShapeDtypeStruct((b, t, A_Q), BF16),
        scratch_shapes=[pltpu.VMEM((A_HEADS, 2 * HEAD_DIM, tq), BF16), pltpu.VMEM((A_HEADS, tq), F32),
                        pltpu.VMEM((A_HEADS, tq), F32), pltpu.VMEM((A_Q, tq), F32)],
        compiler_params=pltpu.CompilerParams(dimension_semantics=("parallel", "parallel"),
                                             vmem_limit_bytes=VMEM_LIMIT),
        name="attn_a",
    )(q_t, k, v_t, g_col)


def _attn_b_kernel(sink_ref, q_ref, kp_ref, km_ref, kn_ref, vp_ref, vm_ref, vn_ref, g_ref, o_ref,
                   kw_ref, vw_ref, *, seq_len):
    tq = q_ref.shape[0]
    w = WINDOW
    i = pl.program_id(1)
    kw_ref[0:w, :] = kp_ref[...]
    kw_ref[w:w + tq, :] = km_ref[...]
    kw_ref[w + tq:2 * w + tq, :] = kn_ref[...]
    vw_ref[0:w, :] = vp_ref[...]
    vw_ref[w:w + tq, :] = vm_ref[...]
    vw_ref[w + tq:2 * w + tq, :] = vn_ref[...]
    qi = lax.broadcasted_iota(jnp.int32, (w, 3 * w), 0)
    kj = lax.broadcasted_iota(jnp.int32, (w, 3 * w), 1)
    band = jnp.abs(kj - w - qi) <= w
    left = _left_lanes(w)
    for c in range(tq // w):
        kpos = (i * (tq // w) + c - 1) * w + kj
        valid = band & (kpos >= 0) & (kpos < seq_len)
        kwin = kw_ref[c * w:(c + 3) * w, :]
        vwin = vw_ref[c * w:(c + 3) * w, :]
        qf = q_ref[c * w:(c + 1) * w, :].astype(F32)
        s0, s1 = qf[:, :LANES], qf[:, LANES:]
        qh = (jnp.where(left, s0, 0.0), pltpu.roll(jnp.where(left, 0.0, s0), HEAD_DIM, 1),
              pltpu.roll(jnp.where(left, s1, 0.0), HEAD_DIM, 1), jnp.where(left, 0.0, s1))
        outs = []
        for hd in range(B_HEADS):
            s = lax.dot_general(qh[hd].astype(BF16), kwin, NT_DIMS, preferred_element_type=F32)
            s = jnp.where(valid, s, MASK_VALUE)
            sink = sink_ref[hd]
            m = jnp.maximum(jnp.max(s, axis=1, keepdims=True), sink)
            p = jnp.exp(s - m)
            denom = jnp.sum(p, axis=1, keepdims=True) + jnp.exp(sink - m)
            outs.append(jnp.dot(p.astype(BF16), vwin, preferred_element_type=F32) / denom)
        slab0 = jnp.where(left, outs[0], pltpu.roll(outs[1], HEAD_DIM, 1))
        slab1 = jnp.where(left, pltpu.roll(outs[2], HEAD_DIM, 1), outs[3])
        o = _rms(jnp.concatenate([slab0, slab1], axis=1), g_ref[...])
        o_ref[c * w:(c + 1) * w, :] = o.astype(BF16)


def _attn_b(q, k, v, sink, g, tq=512):
    b, t, _ = q.shape
    w = WINDOW
    r = tq // w
    nb = t // w
    main = lambda bi, i: (bi, i, 0)
    prev = lambda bi, i: (bi, jnp.maximum(i * r - 1, 0), 0)
    nxt = lambda bi, i: (bi, jnp.minimum(i * r + r, nb - 1), 0)
    kv_specs = [pl.BlockSpec((None, w, LANES), prev), pl.BlockSpec((None, tq, LANES), main),
                pl.BlockSpec((None, w, LANES), nxt)]
    return pl.pallas_call(
        functools.partial(_attn_b_kernel, seq_len=t),
        grid=(b, t // tq),
        in_specs=[pl.BlockSpec(memory_space=pltpu.SMEM), pl.BlockSpec((None, tq, B_Q), main)]
                 + kv_specs + kv_specs + [pl.BlockSpec((1, B_Q), lambda bi, i: (0, 0))],
        out_specs=pl.BlockSpec((None, tq, B_Q), main),
        out_shape=jax.ShapeDtypeStruct((b, t, B_Q), BF16),
        scratch_shapes=[pltpu.VMEM((tq + 2 * w, LANES), BF16), pltpu.VMEM((tq + 2 * w, LANES), BF16)],
        compiler_params=pltpu.CompilerParams(dimension_semantics=("parallel", "parallel"),
                                             vmem_limit_bytes=VMEM_LIMIT),
        name="attn_b",
    )(sink, q, k, k, k, v, v, v, g)


def _ret_fwd_kernel(q_ref, k_ref, v_ref, mask_ref, dq_ref, dk_ref, dec_ref, bm_ref, o_ref, state_ref, *, cps):
    c_len = CHUNK

    @pl.when(pl.program_id(1) == 0)
    def _():
        state_ref[...] = jnp.zeros(state_ref.shape, F32)

    lane_head = lax.broadcasted_iota(jnp.int32, (c_len, C_W), 1) // HEAD_DIM
    for c in range(cps):
        rows = slice(c * c_len, (c + 1) * c_len)
        q = q_ref[rows, :].astype(F32)
        k = k_ref[rows, :]
        v = v_ref[rows, :]
        o = jnp.dot((q * dq_ref[...]).astype(BF16), state_ref[...].astype(BF16), preferred_element_type=F32)
        for hd in range(C_HEADS):
            sel = lane_head == hd
            qm = jnp.where(sel, q, 0.0).astype(BF16)
            s = lax.dot_general(qm, k, NT_DIMS, preferred_element_type=F32) * mask_ref[hd]
            o = o + jnp.where(sel, jnp.dot(s.astype(BF16), v, preferred_element_type=F32), 0.0)
        o_ref[rows, :] = o
        kd = (k.astype(F32) * dk_ref[...]).astype(BF16)
        kv = lax.dot_general(kd, v, TN_DIMS, preferred_element_type=F32)
        state_ref[...] = state_ref[...] * dec_ref[...] + kv * bm_ref[...]


def _ret_bwd_kernel(q_ref, k_ref, v_ref, part_ref, gate_ref, dq_ref, dk_ref, dec_ref, bm_ref, seg_ref, g_ref,
                    o_ref, state_ref, *, cps):
    c_len = CHUNK

    @pl.when(pl.program_id(1) == 0)
    def _():
        state_ref[...] = jnp.zeros(state_ref.shape, F32)

    seg = seg_ref[...]
    for c in reversed(range(cps)):
        rows = slice(c * c_len, (c + 1) * c_len)
        q = q_ref[rows, :].astype(F32)
        k = k_ref[rows, :]
        v = v_ref[rows, :]
        o = part_ref[rows, :] + jnp.dot((q * dq_ref[...]).astype(BF16), state_ref[...].astype(BF16),
                                        preferred_element_type=F32)
        kd = (k.astype(F32) * dk_ref[...]).astype(BF16)
        kv = lax.dot_general(kd, v, TN_DIMS, preferred_element_type=F32)
        state_ref[...] = state_ref[...] * dec_ref[...] + kv * bm_ref[...]
        oc = o - _split_dot(o, seg) * (1.0 / HEAD_DIM)
        var = _split_dot(oc * oc, seg) * (1.0 / HEAD_DIM)
        on = (oc * lax.rsqrt(var + EPS)) * g_ref[...]
        o_ref[rows, :] = (on * gate_ref[rows, :].astype(F32)).astype(BF16)


def _retention(q, k, v, gate, consts, g_gn, cps=4):
    b, t, _ = q.shape
    mask, dqf, dkf, decf, dqb, dkb, decb, bmask, seg = consts
    tr = cps * CHUNK
    ns = t // tr
    fwd = lambda bi, i: (bi, i, 0)
    bwd = lambda bi, i: (bi, ns - 1 - i, 0)
    c2 = lambda bi, i: (0, 0)
    c3 = lambda bi, i: (0, 0, 0)
    params = pltpu.CompilerParams(dimension_semantics=("parallel", "arbitrary"), vmem_limit_bytes=VMEM_LIMIT)
    part = pl.pallas_call(
        functools.partial(_ret_fwd_kernel, cps=cps),
        grid=(b, ns),
        in_specs=[pl.BlockSpec((None, tr, C_W), fwd)] * 3
                 + [pl.BlockSpec((C_HEADS, CHUNK, CHUNK), c3), pl.BlockSpec((CHUNK, C_W), c2),
                    pl.BlockSpec((CHUNK, C_W), c2), pl.BlockSpec((1, C_W), c2), pl.BlockSpec((C_W, C_W), c2)],
        out_specs=pl.BlockSpec((None, tr, C_W), fwd),
        out_shape=jax.ShapeDtypeStruct((b, t, C_W), F32),
        scratch_shapes=[pltpu.VMEM((C_W, C_W), F32)],
        compiler_params=params,
        name="retention_fwd",
    )(q, k, v, mask, dqf, dkf, decf, bmask)
    return pl.pallas_call(
        functools.partial(_ret_bwd_kernel, cps=cps),
        grid=(b, ns),
        in_specs=[pl.BlockSpec((None, tr, C_W), bwd)] * 5
                 + [pl.BlockSpec((CHUNK, C_W), c2), pl.BlockSpec((CHUNK, C_W), c2), pl.BlockSpec((1, C_W), c2),
                    pl.BlockSpec((C_W, C_W), c2), pl.BlockSpec((C_W, C_W), c2), pl.BlockSpec((1, C_W), c2)],
        out_specs=pl.BlockSpec((None, tr, C_W), bwd),
        out_shape=jax.ShapeDtypeStruct((b, t, C_W), BF16),
        scratch_shapes=[pltpu.VMEM((C_W, C_W), F32)],
        compiler_params=params,
        name="retention_bwd",
    )(q, k, v, part, gate, dqb, dkb, decb, bmask, seg, g_gn)


def _mem_proj_kernel(m_ref, g_ref, wk_ref, wv_ref, k_ref, v_ref):
    h = _rms(m_ref[...], g_ref[...]).astype(BF16)
    k_ref[...] = jnp.dot(h, wk_ref[...], preferred_element_type=F32).astype(BF16)
    v_ref[...] = jnp.dot(h, wv_ref[...], preferred_element_type=F32).astype(BF16)


def _mem_proj(mem, g, wk, wv, tm=256):
    n = mem.shape[0]
    row = lambda i: (i, 0)
    const = lambda i: (0, 0)
    return pl.pallas_call(
        _mem_proj_kernel,
        grid=(n // tm,),
        in_specs=[pl.BlockSpec((tm, D_MODEL), row), pl.BlockSpec((1, D_MODEL), const),
                  pl.BlockSpec((D_MODEL, D_MODEL), const), pl.BlockSpec((D_MODEL, D_MODEL), const)],
        out_specs=[pl.BlockSpec((tm, D_MODEL), row)] * 2,
        out_shape=[jax.ShapeDtypeStruct((n, D_MODEL), BF16)] * 2,
        compiler_params=pltpu.CompilerParams(dimension_semantics=("parallel",), vmem_limit_bytes=VMEM_LIMIT),
        name="mem_proj",
    )(mem, g, wk, wv)


def _mid_kernel(x_ref, oa_ref, ob_ref, oc_ref, wo_ref, g_ref, wq_ref, km_ref, vm_ref, wxo_ref, o_ref):
    x1 = (x_ref[...]
          + jnp.dot(oa_ref[...], wo_ref[0:A_Q, :], preferred_element_type=F32)
          + jnp.dot(ob_ref[...], wo_ref[A_Q:A_Q + B_Q, :], preferred_element_type=F32)
          + jnp.dot(oc_ref[...], wo_ref[A_Q + B_Q:, :], preferred_element_type=F32))
    h = _rms(x1, g_ref[...]).astype(BF16)
    q = (jnp.dot(h, wq_ref[...], preferred_element_type=F32) * (X_HEAD_DIM ** -0.5)).astype(BF16)
    heads = []
    for hd in range(X_HEADS):
        cols = slice(hd * X_HEAD_DIM, (hd + 1) * X_HEAD_DIM)
        s = lax.dot_general(q[:, cols], km_ref[:, cols], NT_DIMS, preferred_element_type=F32)
        p = jnp.exp(s - jnp.max(s, axis=1, keepdims=True))
        o = jnp.dot(p.astype(BF16), vm_ref[:, cols], preferred_element_type=F32)
        heads.append((o / jnp.sum(p, axis=1, keepdims=True)).astype(BF16))
    o_ref[...] = x1 + jnp.dot(jnp.concatenate(heads, axis=1), wxo_ref[...], preferred_element_type=F32)


def _mid(x, oa, ob, oc, wo, g, wq, km, vm, wxo, tm=512):
    b, t, _ = x.shape
    mlen = km.shape[1]
    row = lambda bi, i: (bi, i, 0)
    const = lambda bi, i: (0, 0)
    per_b = lambda bi, i: (bi, 0, 0)
    wspec = pl.BlockSpec((D_MODEL, D_MODEL), const)
    return pl.pallas_call(
        _mid_kernel,
        grid=(b, t // tm),
        in_specs=[pl.BlockSpec((None, tm, D_MODEL), row), pl.BlockSpec((None, tm, A_Q), row),
                  pl.BlockSpec((None, tm, B_Q), row), pl.BlockSpec((None, tm, C_W), row),
                  wspec, pl.BlockSpec((1, D_MODEL), const), wspec,
                  pl.BlockSpec((None, mlen, D_MODEL), per_b), pl.BlockSpec((None, mlen, D_MODEL), per_b), wspec],
        out_specs=pl.BlockSpec((None, tm, D_MODEL), row),
        out_shape=jax.ShapeDtypeStruct((b, t, D_MODEL), F32),
        compiler_params=pltpu.CompilerParams(dimension_semantics=("parallel", "parallel"),
                                             vmem_limit_bytes=VMEM_LIMIT),
        name="mid",
    )(x, oa, ob, oc, wo, g, wq, km, vm, wxo)


def _ffn_kernel(x_ref, g_ref, wg_ref, wu_ref, wd_ref, gf_ref, o_ref, h_ref, *, final_norm):
    j = pl.program_id(1)

    @pl.when(j == 0)
    def _():
        x = x_ref[...]
        h_ref[...] = _rms(x, g_ref[...]).astype(BF16)
        o_ref[...] = x

    h = h_ref[...]
    a = jnp.dot(h, wg_ref[...], preferred_element_type=F32)
    u = jnp.dot(h, wu_ref[...], preferred_element_type=F32)
    t = ((a / (1.0 + jnp.exp(-a))) * u).astype(BF16)
    o_ref[...] += jnp.dot(t, wd_ref[...], preferred_element_type=F32)

    if final_norm:
        @pl.when(j == pl.num_programs(1) - 1)
        def _():
            o_ref[...] = _rms(o_ref[...], gf_ref[...])


def _ffn(x, g, wg, wu, wd, gf, final_norm, tm=512, tf=1408):
    n = x.shape[0]
    row = lambda i, j: (i, 0)
    const = lambda i, j: (0, 0)
    return pl.pallas_call(
        functools.partial(_ffn_kernel, final_norm=final_norm),
        grid=(n // tm, D_FF // tf),
        in_specs=[pl.BlockSpec((tm, D_MODEL), row), pl.BlockSpec((1, D_MODEL), const),
                  pl.BlockSpec((D_MODEL, tf), lambda i, j: (0, j)), pl.BlockSpec((D_MODEL, tf), lambda i, j: (0, j)),
                  pl.BlockSpec((tf, D_MODEL), lambda i, j: (j, 0)), pl.BlockSpec((1, D_MODEL), const)],
        out_specs=pl.BlockSpec((tm, D_MODEL), row),
        out_shape=jax.ShapeDtypeStruct((n, D_MODEL), F32),
        scratch_shapes=[pltpu.VMEM((tm, D_MODEL), BF16)],
        compiler_params=pltpu.CompilerParams(dimension_semantics=("parallel", "arbitrary"),
                                             vmem_limit_bytes=VMEM_LIMIT),
        name="ffn",
    )(x, g, wg, wu, wd, gf)


def _rope_freqs(pos, dim):
    inv = ROPE_THETA ** (-jnp.arange(0, dim, 2, dtype=F32) / dim)
    return pos.astype(F32)[:, None] * inv[None, :]


def _rope_tables(t):
    rows = t // GRID_W
    row = jnp.repeat(jnp.arange(rows), GRID_W)
    col = jnp.tile(jnp.arange(GRID_W), rows)
    ang_axial = jnp.concatenate([_rope_freqs(row, HEAD_DIM // 2), _rope_freqs(col, HEAD_DIM // 2)], axis=-1)
    ang_seq = _rope_freqs(jnp.arange(t), HEAD_DIM)

    def lanes(ang):
        cos, sin = jnp.cos(ang), jnp.sin(ang)
        reps = LANES // HEAD_DIM
        return (jnp.tile(jnp.concatenate([cos, cos], axis=-1), (1, reps)),
                jnp.tile(jnp.concatenate([-sin, sin], axis=-1), (1, reps)))

    return lanes(ang_axial) + lanes(ang_seq)


def _retention_consts(p_f, p_b, bmask, seg):
    c = CHUNK
    lgf = -jnp.exp(p_f.astype(F32))
    lgb = -jnp.exp(p_b.astype(F32))
    idx = jnp.arange(c, dtype=F32)
    diff = idx[:, None] - idx[None, :]
    mask = jnp.where(diff >= 0,
                     jnp.exp(lgf[:, None, None] * jnp.maximum(diff, 0.0)),
                     jnp.exp(lgb[:, None, None] * jnp.maximum(-diff, 0.0)))
    lf = jnp.repeat(lgf, HEAD_DIM)[None, :]
    lb = jnp.repeat(lgb, HEAD_DIM)[None, :]
    dqf = jnp.exp(lf * (idx + 1.0)[:, None])
    dkf = jnp.exp(lf * (c - 1 - idx)[:, None])
    decf = jnp.exp(lf * c)
    dqb = jnp.exp(lb * (c - idx)[:, None])
    dkb = jnp.exp(lb * idx[:, None])
    decb = jnp.exp(lb * c)
    return mask, dqf, dkf, decf, dqb, dkb, decb, bmask, seg


def _block_ones(n):
    r = jnp.arange(n) // HEAD_DIM
    return r[:, None] == r[None, :]


def _trunk(x, mem, weights, layer_consts, seg128, g_final):
    b, t, _ = x.shape
    n = b * t
    tabs = _rope_tables(t)
    mem2 = mem.reshape(b * mem.shape[1], D_MODEL)
    xf = x.reshape(n, D_MODEL)
    for l in range(DEPTH):
        wl = weights[l]
        qa, ka, va, qb, kb, vb, qc, kc, vc, gc = _in_proj(
            xf, wl["g_mix"], wl["w_in"], wl["gq"], wl["gk"], tabs, seg128, t)
        r3 = lambda a: a.reshape(b, t, a.shape[-1])
        oa = _attn_a(qa, r3(ka), va, wl["a_out_norm"], b, t)
        ob = _attn_b(r3(qb), r3(kb), r3(vb), wl["b_sink"], wl["b_out_norm"])
        oc = _retention(r3(qc), r3(kc), r3(vc), r3(gc), layer_consts[l], wl["c_gn"])
        km, vm = _mem_proj(mem2, wl["g_mem"], wl["w_xk"], wl["w_xv"])
        km = km.reshape(b, -1, D_MODEL)
        vm = vm.reshape(b, -1, D_MODEL)
        x2 = _mid(xf.reshape(b, t, D_MODEL), oa, ob, oc, wl["w_out"], wl["g_cross"], wl["w_xq"], km, vm,
                  wl["w_xo"])
        xf = _ffn(x2.reshape(n, D_MODEL), wl["g_ffn"], wl["w_gate"], wl["w_up"], wl["w_down"], g_final,
                  final_norm=(l == DEPTH - 1))
    return xf.reshape(b, t, D_MODEL)


def kernel(x_prompt, x_sample, mem_prompt, mem_sample, g_mix, w_in, a_q_norm, a_k_norm, a_out_norm, b_sink,
           b_out_norm, c_decay_fwd, c_decay_bwd, c_gn, w_out, g_cross, g_mem, w_xq, w_xk, w_xv, w_xo, g_ffn,
           w_gate, w_up, w_down, g_final):
    row = lambda a: a.astype(F32).reshape(1, -1)
    reps = LANES // HEAD_DIM
    seg128 = _block_ones(LANES).astype(BF16)
    seg256 = _block_ones(C_W).astype(BF16)
    bmask = _block_ones(C_W).astype(F32)
    weights, layer_consts = [], []
    for l in range(DEPTH):
        weights.append(dict(
            g_mix=row(g_mix[l]), w_in=w_in[l].astype(BF16),
            gq=jnp.tile(row(a_q_norm[l]), (1, reps)), gk=jnp.tile(row(a_k_norm[l]), (1, reps)),
            a_out_norm=a_out_norm[l].astype(F32).reshape(-1, 1), b_sink=b_sink[l].astype(F32), b_out_norm=row(b_out_norm[l]),
            c_gn=row(c_gn[l]), w_out=w_out[l].astype(BF16), g_cross=row(g_cross[l]), g_mem=row(g_mem[l]),
            w_xq=w_xq[l].astype(BF16), w_xk=w_xk[l].astype(BF16), w_xv=w_xv[l].astype(BF16),
            w_xo=w_xo[l].astype(BF16), g_ffn=row(g_ffn[l]), w_gate=w_gate[l].astype(BF16),
            w_up=w_up[l].astype(BF16), w_down=w_down[l].astype(BF16)))
        layer_consts.append(_retention_consts(c_decay_fwd[l], c_decay_bwd[l], bmask, seg256))
    gf = row(g_final)
    y_prompt = _trunk(x_prompt, mem_prompt, weights, layer_consts, seg128, gf)
    y_sample = _trunk(x_sample, mem_sample, weights, layer_consts, seg128, gf)
    return (y_prompt, y_sample)
```
